```python
import math
import jax, jax.numpy as jnp
from jax import lax
import numpy as np

D_MODEL = 1024
BATCH = 2
SEQ = 8192
DEPTH = 1
DEC_BATCH = 32
DEC_SEQ = 2048
PAST_LEN = 128

GRID_W = 64
Q_BLOCK = 128
A_HEADS = 8
A_QK_DIM = 64
A_V_DIM = 2 * A_QK_DIM
B_HEADS = 8
B_KV_HEADS = 2
B_HEAD_DIM = 128
B_GROUP = B_HEADS // B_KV_HEADS
ROPE_THETA = 10000.0
ROPE_AXIS_DIM = B_HEAD_DIM // 2
REL_BUCKETS = 32
REL_MAX_DIST = 128
D_FF = -(-8 * D_MODEL // (3 * 256)) * 256
LN_EPS = 1e-5
RMS_EPS = 1e-6
DEEPNORM_ALPHA = (2 * DEPTH) ** 0.25
DEEPNORM_BETA = (8 * DEPTH) ** -0.25

A_Q_W = A_HEADS * 2 * A_QK_DIM
A_K_W = A_HEADS * 2 * A_QK_DIM
A_V_W = A_HEADS * A_V_DIM
B_Q_W = B_HEADS * B_HEAD_DIM
B_KV_W = B_KV_HEADS * B_HEAD_DIM
GATE_W = D_MODEL
IN_WIDTHS = [A_Q_W, A_K_W, A_V_W, B_Q_W, B_KV_W, B_KV_W, GATE_W, GATE_W]
IN_TOTAL = sum(IN_WIDTHS)
IN_SPLITS = [int(v) for v in np.cumsum(IN_WIDTHS)[:-1]]

kernel_name = "hybrid_diffattn_axialgqa_deepnorm_encoder"


def layer_norm(x, g, b):
    xf = x.astype(jnp.float32)
    mu = jnp.mean(xf, axis=-1, keepdims=True)
    var = jnp.mean(jnp.square(xf - mu), axis=-1, keepdims=True)
    return ((xf - mu) * lax.rsqrt(var + LN_EPS) * g.astype(jnp.float32) + b.astype(jnp.float32)).astype(x.dtype)


def rms_norm(x, g):
    xf = x.astype(jnp.float32)
    ms = jnp.mean(jnp.square(xf), axis=-1, keepdims=True)
    return (xf * lax.rsqrt(ms + RMS_EPS) * g.astype(jnp.float32)).astype(x.dtype)


def t5_bucket(rel):
    nb = REL_BUCKETS // 2
    max_exact = nb // 2
    ret = (rel > 0).astype(jnp.int32) * nb
    n = jnp.abs(rel)
    nf = jnp.maximum(n, 1).astype(jnp.float32)
    large = max_exact + (jnp.log(nf / max_exact) / math.log(REL_MAX_DIST / max_exact)
                         * (nb - max_exact)).astype(jnp.int32)
    large = jnp.minimum(large, nb - 1)
    return ret + jnp.where(n < max_exact, n, large)


def axial_rope_tables(S):
    rows = S // GRID_W
    row = jnp.repeat(jnp.arange(rows), GRID_W).astype(jnp.float32)
    col = jnp.tile(jnp.arange(GRID_W), rows).astype(jnp.float32)
    freqs = ROPE_THETA ** (-jnp.arange(0, ROPE_AXIS_DIM, 2, dtype=jnp.float32) / ROPE_AXIS_DIM)
    ang_r = row[:, None] * freqs[None, :]
    ang_c = col[:, None] * freqs[None, :]
    cos = jnp.concatenate([jnp.cos(ang_r), jnp.cos(ang_r), jnp.cos(ang_c), jnp.cos(ang_c)], axis=-1)
    sin = jnp.concatenate([jnp.sin(ang_r), jnp.sin(ang_r), jnp.sin(ang_c), jnp.sin(ang_c)], axis=-1)
    return cos, sin


def apply_axial_rope(x, cos, sin):
    half = ROPE_AXIS_DIM // 2
    xs = x.reshape(x.shape[:-1] + (2, 2, half))
    rot = jnp.stack([-xs[..., 1, :], xs[..., 0, :]], axis=-2).reshape(x.shape)
    c = cos[None, :, None, :].astype(x.dtype)
    s = sin[None, :, None, :].astype(x.dtype)
    return x * c + rot * s


def diff_attention(qa, ka, va, lam, lam_init, subln_g, rel_bias):
    B, S, _ = qa.shape
    nblk = S // Q_BLOCK
    scale = A_QK_DIM ** -0.5
    q = qa.reshape(B, nblk, Q_BLOCK, A_HEADS, 2, A_QK_DIM).transpose(1, 0, 2, 3, 4, 5)
    k = ka.reshape(B, S, A_HEADS, 2, A_QK_DIM)
    v = va.reshape(B, S, A_HEADS, A_V_DIM)
    kpos = jnp.arange(S, dtype=jnp.int32)

    def block(args):
        qb, start = args
        qpos = start + jnp.arange(Q_BLOCK, dtype=jnp.int32)
        bias = rel_bias[t5_bucket(kpos[None, :] - qpos[:, None])]
        bias = bias.transpose(2, 0, 1).astype(jnp.float32)
        s = jnp.einsum('bqhcd,bkhcd->cbhqk', qb, k).astype(jnp.float32) * scale + bias
        p = jax.nn.softmax(s, axis=-1)
        a = p[0] - lam * p[1]
        return jnp.einsum('bhqk,bkhe->bqhe', a.astype(v.dtype), v)

    starts = jnp.arange(nblk, dtype=jnp.int32) * Q_BLOCK
    o = lax.map(block, (q, starts))
    o = o.transpose(1, 0, 2, 3, 4).reshape(B, S, A_HEADS, A_V_DIM)
    o = rms_norm(o, subln_g) * (1.0 - lam_init)
    return o.reshape(B, S, A_V_W)


def axial_gqa(qb_, kb_, vb_, q_norm_g, k_norm_g):
    B, S, _ = qb_.shape
    nblk = S // Q_BLOCK
    scale = B_HEAD_DIM ** -0.5
    cos, sin = axial_rope_tables(S)
    q = apply_axial_rope(rms_norm(qb_.reshape(B, S, B_HEADS, B_HEAD_DIM), q_norm_g), cos, sin)
    k = apply_axial_rope(rms_norm(kb_.reshape(B, S, B_KV_HEADS, B_HEAD_DIM), k_norm_g), cos, sin)
    v = vb_.reshape(B, S, B_KV_HEADS, B_HEAD_DIM)
    q = q.reshape(B, nblk, Q_BLOCK, B_KV_HEADS, B_GROUP, B_HEAD_DIM).transpose(1, 0, 2, 3, 4, 5)

    def block(qblk):
        s = jnp.einsum('bqkgd,bskd->bkgqs', qblk, k).astype(jnp.float32) * scale
        p = jax.nn.softmax(s, axis=-1)
        return jnp.einsum('bkgqs,bskd->bqkgd', p.astype(v.dtype), v)

    o = lax.map(block, q)
    return o.transpose(1, 0, 2, 3, 4, 5).reshape(B, S, B_Q_W)


def encoder_layer(x, layer_idx, w_in, lambda_q1, lambda_k1, lambda_q2, lambda_k2, subln_g,
                  q_norm_g, k_norm_g, rel_bias, w_proj_a, w_proj_b, w_o, ln1_g, ln1_b,
                  w_gate, w_up, w_down, ln2_g, ln2_b):
    lam_init = 0.8 - 0.6 * math.exp(-0.3 * layer_idx)
    lam = (jnp.exp(jnp.sum(lambda_q1.astype(jnp.float32) * lambda_k1.astype(jnp.float32)))
           - jnp.exp(jnp.sum(lambda_q2.astype(jnp.float32) * lambda_k2.astype(jnp.float32)))
           + lam_init)
    proj = x @ w_in
    qa, ka, va, qb, kb, vb, ga, gb = jnp.split(proj, IN_SPLITS, axis=-1)
    oa = diff_attention(qa, ka, va, lam, lam_init, subln_g, rel_bias)
    ob = axial_gqa(qb, kb, vb, q_norm_g, k_norm_g)
    merged = jax.nn.sigmoid(ga) * (oa @ w_proj_a) + jax.nn.sigmoid(gb) * (ob @ w_proj_b)
    h = layer_norm(DEEPNORM_ALPHA * x + merged @ w_o, ln1_g, ln1_b)
    ffn = (jax.nn.silu(h @ w_gate) * (h @ w_up)) @ w_down
    return layer_norm(DEEPNORM_ALPHA * h + ffn, ln2_g, ln2_b)


def setup_inputs(seed: int = 0) -> dict:
    key = jax.random.key(seed)
    ks = jax.random.split(key, 32)
    f32 = jnp.float32

    def nrm(k, shape, scale):
        return jax.random.normal(k, shape, f32) * scale

    s_in = D_MODEL ** -0.5
    beta = DEEPNORM_BETA
    cols = [
        nrm(ks[2], (DEPTH, D_MODEL, A_Q_W), s_in),
        nrm(ks[3], (DEPTH, D_MODEL, A_K_W), s_in),
        nrm(ks[4], (DEPTH, D_MODEL, A_V_W), s_in * beta),
        nrm(ks[5], (DEPTH, D_MODEL, B_Q_W), s_in),
        nrm(ks[6], (DEPTH, D_MODEL, B_KV_W), s_in),
        nrm(ks[7], (DEPTH, D_MODEL, B_KV_W), s_in * beta),
        nrm(ks[8], (DEPTH, D_MODEL, GATE_W), s_in),
        nrm(ks[9], (DEPTH, D_MODEL, GATE_W), s_in),
    ]
    return {
        "x_prompt": jax.random.normal(ks[0], (BATCH, SEQ, D_MODEL), f32),
        "x_sample": jax.random.normal(ks[1], (DEC_BATCH, DEC_SEQ, D_MODEL), f32),
        "w_in": jnp.concatenate(cols, axis=-1),
        "lambda_q1": nrm(ks[10], (DEPTH, A_QK_DIM), 0.1),
        "lambda_k1": nrm(ks[11], (DEPTH, A_QK_DIM), 0.1),
        "lambda_q2": nrm(ks[12], (DEPTH, A_QK_DIM), 0.1),
        "lambda_k2": nrm(ks[13], (DEPTH, A_QK_DIM), 0.1),
        "subln_g": 1.0 + nrm(ks[14], (DEPTH, A_V_DIM), 0.02),
        "q_norm_g": 1.0 + nrm(ks[15], (DEPTH, B_HEAD_DIM), 0.02),
        "k_norm_g": 1.0 + nrm(ks[16], (DEPTH, B_HEAD_DIM), 0.02),
        "rel_bias": nrm(ks[17], (REL_BUCKETS, A_HEADS), 0.5),
        "w_proj_a": nrm(ks[18], (DEPTH, A_V_W, D_MODEL), A_V_W ** -0.5 * beta),
        "w_proj_b": nrm(ks[19], (DEPTH, B_Q_W, D_MODEL), B_Q_W ** -0.5 * beta),
        "w_o": nrm(ks[20], (DEPTH, D_MODEL, D_MODEL), s_in * beta),
        "ln1_g": 1.0 + nrm(ks[21], (DEPTH, D_MODEL), 0.02),
        "ln1_b": nrm(ks[22], (DEPTH, D_MODEL), 0.02),
        "w_gate": nrm(ks[23], (DEPTH, D_MODEL, D_FF), s_in),
        "w_up": nrm(ks[24], (DEPTH, D_MODEL, D_FF), s_in),
        "w_down": nrm(ks[25], (DEPTH, D_FF, D_MODEL), D_FF ** -0.5 * beta),
        "ln2_g": 1.0 + nrm(ks[26], (DEPTH, D_MODEL), 0.02),
        "ln2_b": nrm(ks[27], (DEPTH, D_MODEL), 0.02),
    }


def reference(x_prompt, x_sample, w_in, lambda_q1, lambda_k1, lambda_q2, lambda_k2, subln_g,
              q_norm_g, k_norm_g, rel_bias, w_proj_a, w_proj_b, w_o, ln1_g, ln1_b,
              w_gate, w_up, w_down, ln2_g, ln2_b):
    def run(x):
        for l in range(DEPTH):
            x = encoder_layer(x, l, w_in[l], lambda_q1[l], lambda_k1[l], lambda_q2[l], lambda_k2[l],
                              subln_g[l], q_norm_g[l], k_norm_g[l], rel_bias, w_proj_a[l],
                              w_proj_b[l], w_o[l], ln1_g[l], ln1_b[l], w_gate[l], w_up[l],
                              w_down[l], ln2_g[l], ln2_b[l])
        return x

    y_prompt = run(x_prompt)
    y_sample = run(x_sample)
    return (y_prompt, y_sample)
```

```python
import functools
import math

import jax
import jax.numpy as jnp
from jax import lax
from jax.experimental import pallas as pl
from jax.experimental.pallas import tpu as pltpu

F32 = jnp.float32
BF16 = jnp.bfloat16

D_MODEL = 1024
GRID_W = 64
HEAD_W = 128
A_HEADS = 8
A_QK_DIM = 64
B_HEADS = 8
B_KV_HEADS = 2
B_GROUP = B_HEADS // B_KV_HEADS
ROPE_THETA = 10000.0
ROPE_AXIS_DIM = HEAD_W // 2
REL_BUCKETS = 32
REL_MAX_DIST = 128
LN_EPS = 1e-5
RMS_EPS = 1e-6
DEPTH = 1
DEEPNORM_ALPHA = (2 * DEPTH) ** 0.25

A_Q_COLS = A_HEADS * HEAD_W
PROJ_A_COLS = 3 * A_Q_COLS
PROJ_B_COLS = (B_HEADS + 2 * B_KV_HEADS) * HEAD_W
PROJ_COLS = PROJ_A_COLS + PROJ_B_COLS
PROJ_TN = PROJ_B_COLS

VMEM_LIMIT_BYTES = 48 * 1024 * 1024

PROJ_TM = 1024
ATTN_A_T = 512
ATTN_B_T = 256
ATTN_B_TK = 512
POST_TM = 512
FFN_TM = 512
BIAS_DELTAS = 5


def _params(n_axes):
    return pltpu.CompilerParams(dimension_semantics=("arbitrary",) * n_axes,
                                vmem_limit_bytes=VMEM_LIMIT_BYTES)


def _resident(shape, index_map):
    return pl.BlockSpec(shape, index_map, pipeline_mode=pl.Buffered(1))


def _in_proj_kernel(x_ref, w_ref, cos_ref, sin_ref, gq_ref, gk_ref, o_ref):
    j = pl.program_id(1)
    res = jnp.dot(x_ref[...].astype(BF16), w_ref[...], preferred_element_type=F32)

    @pl.when(j < 2)
    def _():
        o_ref[...] = res.astype(BF16)

    @pl.when(j == 2)
    def _():
        cos = cos_ref[...]
        sin = sin_ref[...]
        lane = lax.broadcasted_iota(jnp.int32, cos.shape, 1)
        first_half = (lane % ROPE_AXIS_DIM) < (ROPE_AXIS_DIM // 2)
        n_rope = B_HEADS + B_KV_HEADS
        for h in range(PROJ_B_COLS // HEAD_W):
            v = res[:, h * HEAD_W:(h + 1) * HEAD_W]
            if h < n_rope:
                g = gq_ref[...] if h < B_HEADS else gk_ref[...]
                ms = jnp.mean(v * v, axis=-1, keepdims=True)
                vn = v * lax.rsqrt(ms + RMS_EPS) * g
                rot = jnp.where(first_half,
                                pltpu.roll(vn, HEAD_W - ROPE_AXIS_DIM // 2, 1),
                                pltpu.roll(vn, ROPE_AXIS_DIM // 2, 1))
                v = vn * cos + rot * sin
            o_ref[:, h * HEAD_W:(h + 1) * HEAD_W] = v.astype(BF16)


def _in_proj(x2d, w, cos, sin, gq, gk, seq):
    n_tok = x2d.shape[0]
    tm = PROJ_TM
    tiles_per_seq = seq // tm
    return pl.pallas_call(
        _in_proj_kernel,
        grid=(n_tok // tm, PROJ_COLS // PROJ_TN),
        in_specs=[
            pl.BlockSpec((tm, D_MODEL), lambda i, j: (i, 0)),
            pl.BlockSpec((D_MODEL, PROJ_TN), lambda i, j: (0, j)),
            pl.BlockSpec((tm, HEAD_W), lambda i, j: (i % tiles_per_seq, 0)),
            pl.BlockSpec((tm, HEAD_W), lambda i, j: (i % tiles_per_seq, 0)),
            pl.BlockSpec((1, HEAD_W), lambda i, j: (0, 0)),
            pl.BlockSpec((1, HEAD_W), lambda i, j: (0, 0)),
        ],
        out_specs=pl.BlockSpec((tm, PROJ_TN), lambda i, j: (i, j)),
        out_shape=jax.ShapeDtypeStruct((n_tok, PROJ_COLS), BF16),
        compiler_params=_params(2),
        name="in_proj",
    )(x2d, w, cos, sin, gq, gk)


def _bias_tiles_kernel(rb_ref, bucket_ref, o_ref):
    h = pl.program_id(0)
    bucket = bucket_ref[0]
    acc = jnp.zeros(bucket.shape, F32)
    for b in range(REL_BUCKETS):
        acc = jnp.where(bucket == b, rb_ref[b, h], acc)
    o_ref[0, 0] = acc


def _bias_tiles(rel_bias, bucket, t):
    return pl.pallas_call(
        _bias_tiles_kernel,
        grid=(A_HEADS, BIAS_DELTAS),
        in_specs=[
            pl.BlockSpec(memory_space=pltpu.SMEM),
            pl.BlockSpec((1, t, t), lambda h, d: (d, 0, 0)),
        ],
        out_specs=pl.BlockSpec((1, 1, t, t), lambda h, d: (h, d, 0, 0)),
        out_shape=jax.ShapeDtypeStruct((A_HEADS, BIAS_DELTAS, t, t), F32),
        compiler_params=_params(2),
        name="t5_bias_tiles",
    )(rel_bias, bucket)


def _online_softmax_loop(q, k_ref, v_ref, n_chunks, tk, score_fn, m_ref, l_ref, acc_ref):
    m_ref[...] = jnp.full(m_ref.shape, -jnp.inf, F32)
    l_ref[...] = jnp.zeros(l_ref.shape, F32)
    acc_ref[...] = jnp.zeros(acc_ref.shape, F32)

    def body(j, carry):
        start = pl.multiple_of(j * tk, tk)
        ks = k_ref[pl.ds(start, tk), :]
        vs = v_ref[pl.ds(start, tk), :]
        s = lax.dot_general(q, ks, (((1,), (1,)), ((), ())), preferred_element_type=F32)
        s = score_fn(s, j)
        m_prev = m_ref[...]
        m_new = jnp.maximum(m_prev, jnp.max(s, axis=-1, keepdims=True))
        alpha = jnp.exp(m_prev - m_new)
        p = jnp.exp(s - m_new)
        l_ref[...] = alpha * l_ref[...] + jnp.sum(p, axis=-1, keepdims=True)
        acc_ref[...] = alpha * acc_ref[...] + jnp.dot(p.astype(BF16), vs, preferred_element_type=F32)
        m_ref[...] = m_new
        return carry

    lax.fori_loop(0, n_chunks, body, 0)


def _attn_a_kernel(q_ref, k_ref, v_ref, bias_ref, lq1_ref, lk1_ref, lq2_ref, lk2_ref, g_ref,
                   o_ref, m_ref, l_ref, acc_ref, *, n_chunks, lam_init):
    t = q_ref.shape[0]
    i = pl.program_id(2)
    q = q_ref[...] * jnp.asarray(A_QK_DIM ** -0.5, BF16)
    lane = lax.broadcasted_iota(jnp.int32, q.shape, 1)
    zero = jnp.zeros_like(q)
    q2m = jnp.concatenate([jnp.where(lane < A_QK_DIM, q, zero),
                           jnp.where(lane >= A_QK_DIM, q, zero)], axis=0)

    def score_fn(s, j):
        d = jnp.clip(j - i, -(BIAS_DELTAS // 2), BIAS_DELTAS // 2) + BIAS_DELTAS // 2
        b = bias_ref[0, d]
        return (s.reshape(2, t, t) + b[None]).reshape(2 * t, t)

    _online_softmax_loop(q2m, k_ref, v_ref, n_chunks, t, score_fn, m_ref, l_ref, acc_ref)

    lam = (jnp.exp(jnp.sum(lq1_ref[...] * lk1_ref[...], axis=-1, keepdims=True))
           - jnp.exp(jnp.sum(lq2_ref[...] * lk2_ref[...], axis=-1, keepdims=True))
           + lam_init)
    o1 = acc_ref[0:t, :] / l_ref[0:t, :]
    o2 = acc_ref[t:2 * t, :] / l_ref[t:2 * t, :]
    o = o1 - lam * o2
    ms = jnp.mean(o * o, axis=-1, keepdims=True)
    o = (o * lax.rsqrt(ms + RMS_EPS) * g_ref[...]) * (1.0 - lam_init)
    o_ref[...] = o.astype(o_ref.dtype)


def _attn_a(proj, bias, lq1, lk1, lq2, lk2, subln_g, batch, seq, lam_init):
    t = ATTN_A_T
    qt = seq // t
    lam_spec = pl.BlockSpec((1, A_QK_DIM), lambda b, h, i: (0, 0))
    return pl.pallas_call(
        functools.partial(_attn_a_kernel, n_chunks=seq // t, lam_init=lam_init),
        grid=(batch, A_HEADS, qt),
        in_specs=[
            pl.BlockSpec((t, HEAD_W), lambda b, h, i: (b * qt + i, h)),
            pl.BlockSpec((seq, HEAD_W), lambda b, h, i: (b, A_HEADS + h)),
            pl.BlockSpec((seq, HEAD_W), lambda b, h, i: (b, 2 * A_HEADS + h)),
            pl.BlockSpec((1, BIAS_DELTAS, t, t), lambda b, h, i: (h, 0, 0, 0)),
            lam_spec, lam_spec, lam_spec, lam_spec,
            pl.BlockSpec((1, HEAD_W), lambda b, h, i: (0, 0)),
        ],
        out_specs=pl.BlockSpec((t, HEAD_W), lambda b, h, i: (b * qt + i, h)),
        out_shape=jax.ShapeDtypeStruct((batch * seq, A_HEADS * HEAD_W), BF16),
        scratch_shapes=[
            pltpu.VMEM((2 * t, 1), F32),
            pltpu.VMEM((2 * t, 1), F32),
            pltpu.VMEM((2 * t, HEAD_W), F32),
        ],
        compiler_params=_params(3),
        name="attn_a",
    )(proj, proj, proj, bias, lq1, lk1, lq2, lk2, subln_g)


def _attn_b_kernel(q_ref, k_ref, v_ref, o_ref, m_ref, l_ref, acc_ref, *, n_chunks, tk):
    t = q_ref.shape[0]
    q = jnp.concatenate([q_ref[:, h * HEAD_W:(h + 1) * HEAD_W] for h in range(B_GROUP)], axis=0)
    scale = HEAD_W ** -0.5

    _online_softmax_loop(q, k_ref, v_ref, n_chunks, tk, lambda s, j: s * scale,
                         m_ref, l_ref, acc_ref)

    o = acc_ref[...] / l_ref[...]
    for h in range(B_GROUP):
        o_ref[:, h * HEAD_W:(h + 1) * HEAD_W] = o[h * t:(h + 1) * t, :].astype(o_ref.dtype)


def _attn_b(proj, batch, seq):
    t = ATTN_B_T
    tk = ATTN_B_TK
    qt = seq // t
    gw = B_GROUP * HEAD_W
    col0 = PROJ_A_COLS // HEAD_W
    return pl.pallas_call(
        functools.partial(_attn_b_kernel, n_chunks=seq // tk, tk=tk),
        grid=(batch, B_KV_HEADS, qt),
        in_specs=[
            pl.BlockSpec((t, gw), lambda b, g, i: (b * qt + i, PROJ_A_COLS // gw + g)),
            pl.BlockSpec((seq, HEAD_W), lambda b, g, i: (b, col0 + B_HEADS + g)),
            pl.BlockSpec((seq, HEAD_W), lambda b, g, i: (b, col0 + B_HEADS + B_KV_HEADS + g)),
        ],
        out_specs=pl.BlockSpec((t, gw), lambda b, g, i: (b * qt + i, g)),
        out_shape=jax.ShapeDtypeStruct((batch * seq, B_HEADS * HEAD_W), BF16),
        scratch_shapes=[
            pltpu.VMEM((B_GROUP * t, 1), F32),
            pltpu.VMEM((B_GROUP * t, 1), F32),
            pltpu.VMEM((B_GROUP * t, HEAD_W), F32),
        ],
        compiler_params=_params(3),
        name="attn_b",
    )(proj, proj, proj)


def _layer_norm(x, g, b):
    mu = jnp.mean(x, axis=-1, keepdims=True)
    xc = x - mu
    var = jnp.mean(xc * xc, axis=-1, keepdims=True)
    return xc * lax.rsqrt(var + LN_EPS) * g + b


def _post_attn_kernel(x_ref, oa_ref, ob_ref, wga_ref, wgb_ref, wa_ref, wb_ref, wo_ref,
                      g_ref, b_ref, h_ref):
    x = x_ref[...]
    xb = x.astype(BF16)
    ga = jnp.dot(xb, wga_ref[...], preferred_element_type=F32)
    gb = jnp.dot(xb, wgb_ref[...], preferred_element_type=F32)
    ta = jnp.dot(oa_ref[...], wa_ref[...], preferred_element_type=F32)
    tb = jnp.dot(ob_ref[...], wb_ref[...], preferred_element_type=F32)
    merged = jax.nn.sigmoid(ga) * ta + jax.nn.sigmoid(gb) * tb
    mo = jnp.dot(merged.astype(BF16), wo_ref[...], preferred_element_type=F32)
    h_ref[...] = _layer_norm(DEEPNORM_ALPHA * x + mo, g_ref[...], b_ref[...])


def _post_attn(x2d, oa, ob, wga, wgb, wa, wb, wo, g, b):
    n_tok = x2d.shape[0]
    tm = POST_TM
    row = lambda i: (i, 0)
    const = lambda i: (0, 0)
    wspec = _resident((D_MODEL, D_MODEL), const)
    vspec = pl.BlockSpec((1, D_MODEL), const)
    return pl.pallas_call(
        _post_attn_kernel,
        grid=(n_tok // tm,),
        in_specs=[
            pl.BlockSpec((tm, D_MODEL), row),
            pl.BlockSpec((tm, D_MODEL), row),
            pl.BlockSpec((tm, D_MODEL), row),
            wspec, wspec, wspec, wspec, wspec, vspec, vspec,
        ],
        out_specs=pl.BlockSpec((tm, D_MODEL), row),
        out_shape=jax.ShapeDtypeStruct((n_tok, D_MODEL), F32),
        compiler_params=_params(1),
        name="post_attn",
    )(x2d, oa, ob, wga, wgb, wa, wb, wo, g, b)


def _ffn_kernel(h_ref, wg_ref, wu_ref, wd_ref, g_ref, b_ref, y_ref):
    h = h_ref[...]
    hb = h.astype(BF16)
    gate = jnp.dot(hb, wg_ref[...], preferred_element_type=F32)
    up = jnp.dot(hb, wu_ref[...], preferred_element_type=F32)
    act = jax.nn.silu(gate) * up
    f = jnp.dot(act.astype(BF16), wd_ref[...], preferred_element_type=F32)
    y_ref[...] = _layer_norm(DEEPNORM_ALPHA * h + f, g_ref[...], b_ref[...])


def _ffn(h2d, wg, wu, wd, g, b):
    n_tok = h2d.shape[0]
    d_ff = wg.shape[1]
    tm = FFN_TM
    row = lambda i: (i, 0)
    const = lambda i: (0, 0)
    vspec = pl.BlockSpec((1, D_MODEL), const)
    return pl.pallas_call(
        _ffn_kernel,
        grid=(n_tok // tm,),
        in_specs=[
            pl.BlockSpec((tm, D_MODEL), row),
            _resident((D_MODEL, d_ff), const),
            _resident((D_MODEL, d_ff), const),
            _resident((d_ff, D_MODEL), const),
            vspec, vspec,
        ],
        out_specs=pl.BlockSpec((tm, D_MODEL), row),
        out_shape=jax.ShapeDtypeStruct((n_tok, D_MODEL), F32),
        compiler_params=_params(1),
        name="ffn",
    )(h2d, wg, wu, wd, g, b)


def _t5_bucket(rel):
    nb = REL_BUCKETS // 2
    max_exact = nb // 2
    ret = (rel > 0).astype(jnp.int32) * nb
    n = jnp.abs(rel)
    nf = jnp.maximum(n, 1).astype(F32)
    large = max_exact + (jnp.log(nf / max_exact) / math.log(REL_MAX_DIST / max_exact)
                         * (nb - max_exact)).astype(jnp.int32)
    large = jnp.minimum(large, nb - 1)
    return ret + jnp.where(n < max_exact, n, large)


def _bucket_tiles(t):
    r = jnp.arange(t, dtype=jnp.int32)[None, :, None]
    c = jnp.arange(t, dtype=jnp.int32)[None, None, :]
    d = (jnp.arange(BIAS_DELTAS, dtype=jnp.int32) - BIAS_DELTAS // 2)[:, None, None] * t
    return _t5_bucket(c - r + d)


def _rope_tables(seq):
    tok = jnp.arange(seq)
    row = (tok // GRID_W).astype(F32)
    col = (tok % GRID_W).astype(F32)
    freqs = ROPE_THETA ** (-jnp.arange(0, ROPE_AXIS_DIM, 2, dtype=F32) / ROPE_AXIS_DIM)
    ang_r = row[:, None] * freqs[None, :]
    ang_c = col[:, None] * freqs[None, :]
    cos = jnp.concatenate([jnp.cos(ang_r), jnp.cos(ang_r), jnp.cos(ang_c), jnp.cos(ang_c)], axis=-1)
    sin = jnp.concatenate([-jnp.sin(ang_r), jnp.sin(ang_r), -jnp.sin(ang_c), jnp.sin(ang_c)], axis=-1)
    return cos, sin


def kernel(x_prompt, x_sample, w_in, lambda_q1, lambda_k1, lambda_q2, lambda_k2, subln_g,
           q_norm_g, k_norm_g, rel_bias, w_proj_a, w_proj_b, w_o, ln1_g, ln1_b,
           w_gate, w_up, w_down, ln2_g, ln2_b):
    assert w_in.shape[0] == DEPTH == 1
    assert ATTN_A_T >= REL_MAX_DIST
    lam_init = 0.8 - 0.6 * math.exp(-0.3 * 0)

    w_main = w_in[0][:, :PROJ_COLS].astype(BF16)
    wga = w_in[0][:, PROJ_COLS:PROJ_COLS + D_MODEL].astype(BF16)
    wgb = w_in[0][:, PROJ_COLS + D_MODEL:].astype(BF16)
    wa = w_proj_a[0].astype(BF16)
    wb = w_proj_b[0].astype(BF16)
    wo = w_o[0].astype(BF16)
    wg = w_gate[0].astype(BF16)
    wu = w_up[0].astype(BF16)
    wd = w_down[0].astype(BF16)

    bias = _bias_tiles(rel_bias, _bucket_tiles(ATTN_A_T), ATTN_A_T)

    def run(x):
        batch, seq, _ = x.shape
        x2d = x.reshape(batch * seq, D_MODEL)
        cos, sin = _rope_tables(seq)
        proj = _in_proj(x2d, w_main, cos, sin, q_norm_g, k_norm_g, seq)
        oa = _attn_a(proj, bias, lambda_q1, lambda_k1, lambda_q2, lambda_k2, subln_g,
                     batch, seq, lam_init)
        ob = _attn_b(proj, batch, seq)
        h = _post_attn(x2d, oa, ob, wga, wgb, wa, wb, wo, ln1_g, ln1_b)
        y = _ffn(h, wg, wu, wd, ln2_g, ln2_b)
        return y.reshape(batch, seq, D_MODEL)

    return (run(x_prompt), run(x_sample))
```

```python
import functools
import math

import jax
import jax.numpy as jnp
from jax import lax
from jax.experimental import pallas as pl
from jax.experimental.pallas import tpu as pltpu

F32 = jnp.float32
BF16 = jnp.bfloat16

D_MODEL = 1024
GRID_W = 64
HEAD_W = 128
A_HEADS = 8
A_QK_DIM = 64
B_HEADS = 8
B_KV_HEADS = 2
B_GROUP = B_HEADS // B_KV_HEADS
ROPE_THETA = 10000.0
ROPE_AXIS_DIM = HEAD_W // 2
REL_BUCKETS = 32
REL_MAX_DIST = 128
LN_EPS = 1e-5
RMS_EPS = 1e-6
DEPTH = 1
DEEPNORM_ALPHA = (2 * DEPTH) ** 0.25
LOG2E = math.log2(math.e)
A_Q_SCALE = A_QK_DIM ** -0.5 * LOG2E
B_Q_SCALE = HEAD_W ** -0.5 * LOG2E

A_Q_COLS = A_HEADS * HEAD_W
PROJ_A_COLS = 3 * A_Q_COLS
PROJ_B_COLS = (B_HEADS + 2 * B_KV_HEADS) * HEAD_W
PROJ_COLS = PROJ_A_COLS + PROJ_B_COLS
PROJ_TN = PROJ_B_COLS

VMEM_LIMIT_BYTES = 48 * 1024 * 1024

PROJ_TM = 1024
ATTN_A_T = 512
ATTN_B_T = 256
ATTN_B_TK = 512
POST_TM = 512
FFN_TM = 512
ROW_BLOCK = 256
BIAS_DELTAS = 5


def _params(n_axes):
    return pltpu.CompilerParams(dimension_semantics=("arbitrary",) * n_axes,
                                vmem_limit_bytes=VMEM_LIMIT_BYTES)


def _resident(shape, index_map):
    return pl.BlockSpec(shape, index_map, pipeline_mode=pl.Buffered(1))


def _in_proj_kernel(x_ref, w_ref, cs_ref, cos_ref, sin_ref, gq_ref, gk_ref, o_ref):
    j = pl.program_id(1)
    res = jnp.dot(x_ref[...].astype(BF16), w_ref[...], preferred_element_type=F32)

    @pl.when(j < 2)
    def _():
        o_ref[...] = (res * cs_ref[...]).astype(BF16)

    @pl.when(j == 2)
    def _():
        cos = cos_ref[...]
        sin = sin_ref[...]
        lane = lax.broadcasted_iota(jnp.int32, cos.shape, 1)
        first_half = (lane % ROPE_AXIS_DIM) < (ROPE_AXIS_DIM // 2)
        n_rope = B_HEADS + B_KV_HEADS
        for h in range(PROJ_B_COLS // HEAD_W):
            v = res[:, h * HEAD_W:(h + 1) * HEAD_W]
            if h < n_rope:
                g = gq_ref[...] if h < B_HEADS else gk_ref[...]
                ms = jnp.mean(v * v, axis=-1, keepdims=True)
                vn = v * lax.rsqrt(ms + RMS_EPS) * g
                rot = jnp.where(first_half,
                                pltpu.roll(vn, HEAD_W - ROPE_AXIS_DIM // 2, 1),
                                pltpu.roll(vn, ROPE_AXIS_DIM // 2, 1))
                v = vn * cos + rot * sin
                if h < B_HEADS:
                    v = v * B_Q_SCALE
            o_ref[:, h * HEAD_W:(h + 1) * HEAD_W] = v.astype(BF16)


def _in_proj(x2d, w, col_scale, cos, sin, gq, gk, seq):
    n_tok = x2d.shape[0]
    tm = PROJ_TM
    tiles_per_seq = seq // tm
    return pl.pallas_call(
        _in_proj_kernel,
        grid=(n_tok // tm, PROJ_COLS // PROJ_TN),
        in_specs=[
            pl.BlockSpec((tm, D_MODEL), lambda i, j: (i, 0)),
            pl.BlockSpec((D_MODEL, PROJ_TN), lambda i, j: (0, j)),
            pl.BlockSpec((1, PROJ_TN), lambda i, j: (0, j)),
            pl.BlockSpec((tm, HEAD_W), lambda i, j: (i % tiles_per_seq, 0)),
            pl.BlockSpec((tm, HEAD_W), lambda i, j: (i % tiles_per_seq, 0)),
            pl.BlockSpec((1, HEAD_W), lambda i, j: (0, 0)),
            pl.BlockSpec((1, HEAD_W), lambda i, j: (0, 0)),
        ],
        out_specs=pl.BlockSpec((tm, PROJ_TN), lambda i, j: (i, j)),
        out_shape=jax.ShapeDtypeStruct((n_tok, PROJ_COLS), BF16),
        compiler_params=_params(2),
        name="in_proj",
    )(x2d, w, col_scale, cos, sin, gq, gk)


def _bias_tiles_kernel(rb_ref, bucket_ref, o_ref):
    h = pl.program_id(0)
    bucket = bucket_ref[0]
    acc = jnp.zeros(bucket.shape, F32)
    for b in range(REL_BUCKETS):
        acc = jnp.where(bucket == b, rb_ref[b, h], acc)
    o_ref[0, 0] = acc * LOG2E


def _bias_tiles(rel_bias, bucket, t):
    return pl.pallas_call(
        _bias_tiles_kernel,
        grid=(A_HEADS, BIAS_DELTAS),
        in_specs=[
            pl.BlockSpec(memory_space=pltpu.SMEM),
            pl.BlockSpec((1, t, t), lambda h, d: (d, 0, 0)),
        ],
        out_specs=pl.BlockSpec((1, 1, t, t), lambda h, d: (h, d, 0, 0)),
        out_shape=jax.ShapeDtypeStruct((A_HEADS, BIAS_DELTAS, t, t), F32),
        compiler_params=_params(2),
        name="t5_bias_tiles",
    )(rel_bias, bucket)


def _online_softmax_loop(q, k_ref, v_ref, n_chunks, tk, score_fn, m_ref, l_ref, acc_ref):
    m_ref[...] = jnp.full(m_ref.shape, -jnp.inf, F32)
    l_ref[...] = jnp.zeros(l_ref.shape, F32)
    acc_ref[...] = jnp.zeros(acc_ref.shape, F32)

    n_rows = q.shape[0]
    lane_blocks = tk // HEAD_W

    def body(j, carry):
        start = pl.multiple_of(j * tk, tk)
        ks = k_ref[pl.ds(start, tk), :]
        vs = v_ref[pl.ds(start, tk), :]
        s_all = lax.dot_general(q, ks, (((1,), (1,)), ((), ())), preferred_element_type=F32)
        for r0 in range(0, n_rows, ROW_BLOCK):
            rows = slice(r0, r0 + ROW_BLOCK)
            s = score_fn(s_all[rows], j, r0)
            m_prev = m_ref[rows, :]
            m_new = jnp.maximum(m_prev, jnp.max(s, axis=-1, keepdims=True))
            alpha = jnp.exp2(m_prev - m_new)
            p = jnp.exp2(s - jnp.tile(m_new, (1, lane_blocks)))
            p_sum = p[:, 0:HEAD_W]
            for c in range(1, lane_blocks):
                p_sum = p_sum + p[:, c * HEAD_W:(c + 1) * HEAD_W]
            l_ref[rows, :] = alpha * l_ref[rows, :] + p_sum
            acc_ref[rows, :] = alpha * acc_ref[rows, :] + jnp.dot(
                p.astype(BF16), vs, preferred_element_type=F32)
            m_ref[rows, :] = m_new
        return carry

    lax.fori_loop(0, n_chunks, body, 0)


def _attn_a_kernel(q_ref, k_ref, v_ref, bias_ref, lq1_ref, lk1_ref, lq2_ref, lk2_ref, g_ref,
                   o_ref, m_ref, l_ref, acc_ref, *, n_chunks, lam_init):
    t = q_ref.shape[0]
    i = pl.program_id(2)
    q = q_ref[...]
    lane = lax.broadcasted_iota(jnp.int32, q.shape, 1)
    zero = jnp.zeros_like(q)
    q2m = jnp.concatenate([jnp.where(lane < A_QK_DIM, q, zero),
                           jnp.where(lane >= A_QK_DIM, q, zero)], axis=0)

    def score_fn(s, j, r0):
        d = jnp.clip(j - i, -(BIAS_DELTAS // 2), BIAS_DELTAS // 2) + BIAS_DELTAS // 2
        q0 = r0 % t
        return s + bias_ref[0, d, q0:q0 + ROW_BLOCK, :]

    _online_softmax_loop(q2m, k_ref, v_ref, n_chunks, t, score_fn, m_ref, l_ref, acc_ref)

    lam = (jnp.exp(jnp.sum(lq1_ref[...] * lk1_ref[...], axis=-1, keepdims=True))
           - jnp.exp(jnp.sum(lq2_ref[...] * lk2_ref[...], axis=-1, keepdims=True))
           + lam_init)
    l = jnp.sum(l_ref[...], axis=-1, keepdims=True)
    o1 = acc_ref[0:t, :] / l[0:t]
    o2 = acc_ref[t:2 * t, :] / l[t:2 * t]
    o = o1 - lam * o2
    ms = jnp.mean(o * o, axis=-1, keepdims=True)
    o = (o * lax.rsqrt(ms + RMS_EPS) * g_ref[...]) * (1.0 - lam_init)
    o_ref[...] = o.astype(o_ref.dtype)


def _attn_a(proj, bias, lq1, lk1, lq2, lk2, subln_g, batch, seq, lam_init):
    t = ATTN_A_T
    qt = seq // t
    lam_spec = pl.BlockSpec((1, A_QK_DIM), lambda b, h, i: (0, 0))
    return pl.pallas_call(
        functools.partial(_attn_a_kernel, n_chunks=seq // t, lam_init=lam_init),
        grid=(batch, A_HEADS, qt),
        in_specs=[
            pl.BlockSpec((t, HEAD_W), lambda b, h, i: (b * qt + i, h)),
            pl.BlockSpec((seq, HEAD_W), lambda b, h, i: (b, A_HEADS + h)),
            pl.BlockSpec((seq, HEAD_W), lambda b, h, i: (b, 2 * A_HEADS + h)),
            pl.BlockSpec((1, BIAS_DELTAS, t, t), lambda b, h, i: (h, 0, 0, 0)),
            lam_spec, lam_spec, lam_spec, lam_spec,
            pl.BlockSpec((1, HEAD_W), lambda b, h, i: (0, 0)),
        ],
        out_specs=pl.BlockSpec((t, HEAD_W), lambda b, h, i: (b * qt + i, h)),
        out_shape=jax.ShapeDtypeStruct((batch * seq, A_HEADS * HEAD_W), BF16),
        scratch_shapes=[
            pltpu.VMEM((2 * t, HEAD_W), F32),
            pltpu.VMEM((2 * t, HEAD_W), F32),
            pltpu.VMEM((2 * t, HEAD_W), F32),
        ],
        compiler_params=_params(3),
        name="attn_a",
    )(proj, proj, proj, bias, lq1, lk1, lq2, lk2, subln_g)


def _attn_b_kernel(q_ref, k_ref, v_ref, o_ref, m_ref, l_ref, acc_ref, *, n_chunks, tk):
    t = q_ref.shape[0]
    q = jnp.concatenate([q_ref[:, h * HEAD_W:(h + 1) * HEAD_W] for h in range(B_GROUP)], axis=0)

    _online_softmax_loop(q, k_ref, v_ref, n_chunks, tk, lambda s, j, r0: s,
                         m_ref, l_ref, acc_ref)

    o = acc_ref[...] / jnp.sum(l_ref[...], axis=-1, keepdims=True)
    for h in range(B_GROUP):
        o_ref[:, h * HEAD_W:(h + 1) * HEAD_W] = o[h * t:(h + 1) * t, :].astype(o_ref.dtype)


def _attn_b(proj, batch, seq):
    t = ATTN_B_T
    tk = ATTN_B_TK
    qt = seq // t
    gw = B_GROUP * HEAD_W
    col0 = PROJ_A_COLS // HEAD_W
    return pl.pallas_call(
        functools.partial(_attn_b_kernel, n_chunks=seq // tk, tk=tk),
        grid=(batch, B_KV_HEADS, qt),
        in_specs=[
            pl.BlockSpec((t, gw), lambda b, g, i: (b * qt + i, PROJ_A_COLS // gw + g)),
            pl.BlockSpec((seq, HEAD_W), lambda b, g, i: (b, col0 + B_HEADS + g)),
            pl.BlockSpec((seq, HEAD_W), lambda b, g, i: (b, col0 + B_HEADS + B_KV_HEADS + g)),
        ],
        out_specs=pl.BlockSpec((t, gw), lambda b, g, i: (b * qt + i, g)),
        out_shape=jax.ShapeDtypeStruct((batch * seq, B_HEADS * HEAD_W), BF16),
        scratch_shapes=[
            pltpu.VMEM((B_GROUP * t, HEAD_W), F32),
            pltpu.VMEM((B_GROUP * t, HEAD_W), F32),
            pltpu.VMEM((B_GROUP * t, HEAD_W), F32),
        ],
        compiler_params=_params(3),
        name="attn_b",
    )(proj, proj, proj)


def _layer_norm(x, g, b):
    mu = jnp.mean(x, axis=-1, keepdims=True)
    xc = x - mu
    var = jnp.mean(xc * xc, axis=-1, keepdims=True)
    return xc * lax.rsqrt(var + LN_EPS) * g + b


def _post_attn_kernel(x_ref, oa_ref, ob_ref, wga_ref, wgb_ref, wa_ref, wb_ref, wo_ref,
                      g_ref, b_ref, h_ref):
    x = x_ref[...]
    xb = x.astype(BF16)
    ga = jnp.dot(xb, wga_ref[...], preferred_element_type=F32)
    gb = jnp.dot(xb, wgb_ref[...], preferred_element_type=F32)
    ta = jnp.dot(oa_ref[...], wa_ref[...], preferred_element_type=F32)
    tb = jnp.dot(ob_ref[...], wb_ref[...], preferred_element_type=F32)
    merged = jax.nn.sigmoid(ga) * ta + jax.nn.sigmoid(gb) * tb
    mo = jnp.dot(merged.astype(BF16), wo_ref[...], preferred_element_type=F32)
    h_ref[...] = _layer_norm(DEEPNORM_ALPHA * x + mo, g_ref[...], b_ref[...])


def _post_attn(x2d, oa, ob, wga, wgb, wa, wb, wo, g, b):
    n_tok = x2d.shape[0]
    tm = POST_TM
    row = lambda i: (i, 0)
    const = lambda i: (0, 0)
    wspec = _resident((D_MODEL, D_MODEL), const)
    vspec = pl.BlockSpec((1, D_MODEL), const)
    return pl.pallas_call(
        _post_attn_kernel,
        grid=(n_tok // tm,),
        in_specs=[
            pl.BlockSpec((tm, D_MODEL), row),
            pl.BlockSpec((tm, D_MODEL), row),
            pl.BlockSpec((tm, D_MODEL), row),
            wspec, wspec, wspec, wspec, wspec, vspec, vspec,
        ],
        out_specs=pl.BlockSpec((tm, D_MODEL), row),
        out_shape=jax.ShapeDtypeStruct((n_tok, D_MODEL), F32),
        compiler_params=_params(1),
        name="post_attn",
    )(x2d, oa, ob, wga, wgb, wa, wb, wo, g, b)


def _ffn_kernel(h_ref, wg_ref, wu_ref, wd_ref, g_ref, b_ref, y_ref):
    h = h_ref[...]
    hb = h.astype(BF16)
    gate = jnp.dot(hb, wg_ref[...], preferred_element_type=F32)
    up = jnp.dot(hb, wu_ref[...], preferred_element_type=F32)
    act = jax.nn.silu(gate) * up
    f = jnp.dot(act.astype(BF16), wd_ref[...], preferred_element_type=F32)
    y_ref[...] = _layer_norm(DEEPNORM_ALPHA * h + f, g_ref[...], b_ref[...])


def _ffn(h2d, wg, wu, wd, g, b):
    n_tok = h2d.shape[0]
    d_ff = wg.shape[1]
    tm = FFN_TM
    row = lambda i: (i, 0)
    const = lambda i: (0, 0)
    vspec = pl.BlockSpec((1, D_MODEL), const)
    return pl.pallas_call(
        _ffn_kernel,
        grid=(n_tok // tm,),
        in_specs=[
            pl.BlockSpec((tm, D_MODEL), row),
            _resident((D_MODEL, d_ff), const),
            _resident((D_MODEL, d_ff), const),
            _resident((d_ff, D_MODEL), const),
            vspec, vspec,
        ],
        out_specs=pl.BlockSpec((tm, D_MODEL), row),
        out_shape=jax.ShapeDtypeStruct((n_tok, D_MODEL), F32),
        compiler_params=_params(1),
        name="ffn",
    )(h2d, wg, wu, wd, g, b)


def _t5_bucket(rel):
    nb = REL_BUCKETS // 2
    max_exact = nb // 2
    ret = (rel > 0).astype(jnp.int32) * nb
    n = jnp.abs(rel)
    nf = jnp.maximum(n, 1).astype(F32)
    large = max_exact + (jnp.log(nf / max_exact) / math.log(REL_MAX_DIST / max_exact)
                         * (nb - max_exact)).astype(jnp.int32)
    large = jnp.minimum(large, nb - 1)
    return ret + jnp.where(n < max_exact, n, large)


def _bucket_tiles(t):
    r = jnp.arange(t, dtype=jnp.int32)[None, :, None]
    c = jnp.arange(t, dtype=jnp.int32)[None, None, :]
    d = (jnp.arange(BIAS_DELTAS, dtype=jnp.int32) - BIAS_DELTAS // 2)[:, None, None] * t
    return _t5_bucket(c - r + d)


def _rope_tables(seq):
    tok = jnp.arange(seq)
    row = (tok // GRID_W).astype(F32)
    col = (tok % GRID_W).astype(F32)
    freqs = ROPE_THETA ** (-jnp.arange(0, ROPE_AXIS_DIM, 2, dtype=F32) / ROPE_AXIS_DIM)
    ang_r = row[:, None] * freqs[None, :]
    ang_c = col[:, None] * freqs[None, :]
    cos = jnp.concatenate([jnp.cos(ang_r), jnp.cos(ang_r), jnp.cos(ang_c), jnp.cos(ang_c)], axis=-1)
    sin = jnp.concatenate([-jnp.sin(ang_r), jnp.sin(ang_r), -jnp.sin(ang_c), jnp.sin(ang_c)], axis=-1)
    return cos, sin


def kernel(x_prompt, x_sample, w_in, lambda_q1, lambda_k1, lambda_q2, lambda_k2, subln_g,
           q_norm_g, k_norm_g, rel_bias, w_proj_a, w_proj_b, w_o, ln1_g, ln1_b,
           w_gate, w_up, w_down, ln2_g, ln2_b):
    assert w_in.shape[0] == DEPTH == 1
    assert ATTN_A_T >= REL_MAX_DIST
    lam_init = 0.8 - 0.6 * math.exp(-0.3 * 0)

    w_main = w_in[0][:, :PROJ_COLS].astype(BF16)
    wga = w_in[0][:, PROJ_COLS:PROJ_COLS + D_MODEL].astype(BF16)
    wgb = w_in[0][:, PROJ_COLS + D_MODEL:].astype(BF16)
    wa = w_proj_a[0].astype(BF16)
    wb = w_proj_b[0].astype(BF16)
    wo = w_o[0].astype(BF16)
    wg = w_gate[0].astype(BF16)
    wu = w_up[0].astype(BF16)
    wd = w_down[0].astype(BF16)

    col_scale = jnp.concatenate([jnp.full((1, A_Q_COLS), A_Q_SCALE, F32),
                                 jnp.ones((1, PROJ_COLS - A_Q_COLS), F32)], axis=1)
    bias = _bias_tiles(rel_bias, _bucket_tiles(ATTN_A_T), ATTN_A_T)

    def run(x):
        batch, seq, _ = x.shape
        x2d = x.reshape(batch * seq, D_MODEL)
        cos, sin = _rope_tables(seq)
        proj = _in_proj(x2d, w_main, col_scale, cos, sin, q_norm_g, k_norm_g, seq)
        oa = _attn_a(proj, bias, lambda_q1, lambda_k1, lambda_q2, lambda_k2, subln_g,
                     batch, seq, lam_init)
        ob = _attn_b(proj, batch, seq)
        h = _post_attn(x2d, oa, ob, wga, wgb, wa, wb, wo, ln1_g, ln1_b)
        y = _ffn(h, wg, wu, wd, ln2_g, ln2_b)
        return y.reshape(batch, seq, D_MODEL)

    return (run(x_prompt), run(x_sample))
```

```python
import functools
import math

import jax
import jax.numpy as jnp
from jax import lax
from jax.experimental import pallas as pl
from jax.experimental.pallas import tpu as pltpu

F32 = jnp.float32
BF16 = jnp.bfloat16

D_MODEL = 1024
GRID_W = 64
HEAD_W = 128
A_HEADS = 8
A_QK_DIM = 64
B_HEADS = 8
B_KV_HEADS = 2
B_GROUP = B_HEADS // B_KV_HEADS
ROPE_THETA = 10000.0
ROPE_AXIS_DIM = HEAD_W // 2
REL_BUCKETS = 32
REL_MAX_DIST = 128
LN_EPS = 1e-5
RMS_EPS = 1e-6
DEPTH = 1
DEEPNORM_ALPHA = (2 * DEPTH) ** 0.25
LOG2E = math.log2(math.e)
A_Q_SCALE = A_QK_DIM ** -0.5 * LOG2E
B_Q_SCALE = HEAD_W ** -0.5 * LOG2E

A_Q_COLS = A_HEADS * HEAD_W
PROJ_A_COLS = 3 * A_Q_COLS
PROJ_B_COLS = (B_HEADS + 2 * B_KV_HEADS) * HEAD_W
PROJ_COLS = PROJ_A_COLS + PROJ_B_COLS
PROJ_TN = PROJ_B_COLS

VMEM_LIMIT_BYTES = 48 * 1024 * 1024

PROJ_TM = 1024
ATTN_A_T = 512
ATTN_B_T = 256
ATTN_B_TK = 512
POST_TM = 512
FFN_TM = 512
ROW_BLOCK = 256
CHUNK_UNROLL = 4
BIAS_DELTAS = 5


def _params(n_axes):
    return pltpu.CompilerParams(dimension_semantics=("arbitrary",) * n_axes,
                                vmem_limit_bytes=VMEM_LIMIT_BYTES)


def _resident(shape, index_map):
    return pl.BlockSpec(shape, index_map, pipeline_mode=pl.Buffered(1))


def _in_proj_kernel(x_ref, w_ref, cs_ref, cos_ref, sin_ref, gq_ref, gk_ref, o_ref):
    j = pl.program_id(1)
    res = jnp.dot(x_ref[...].astype(BF16), w_ref[...], preferred_element_type=F32)

    @pl.when(j < 2)
    def _():
        o_ref[...] = (res * cs_ref[...]).astype(BF16)

    @pl.when(j == 2)
    def _():
        cos = cos_ref[...]
        sin = sin_ref[...]
        lane = lax.broadcasted_iota(jnp.int32, cos.shape, 1)
        first_half = (lane % ROPE_AXIS_DIM) < (ROPE_AXIS_DIM // 2)
        n_rope = B_HEADS + B_KV_HEADS
        for h in range(PROJ_B_COLS // HEAD_W):
            v = res[:, h * HEAD_W:(h + 1) * HEAD_W]
            if h < n_rope:
                g = gq_ref[...] if h < B_HEADS else gk_ref[...]
                ms = jnp.mean(v * v, axis=-1, keepdims=True)
                vn = v * lax.rsqrt(ms + RMS_EPS) * g
                rot = jnp.where(first_half,
                                pltpu.roll(vn, HEAD_W - ROPE_AXIS_DIM // 2, 1),
                                pltpu.roll(vn, ROPE_AXIS_DIM // 2, 1))
                v = vn * cos + rot * sin
                if h < B_HEADS:
                    v = v * B_Q_SCALE
            o_ref[:, h * HEAD_W:(h + 1) * HEAD_W] = v.astype(BF16)


def _in_proj(x2d, w, col_scale, cos, sin, gq, gk, seq):
    n_tok = x2d.shape[0]
    tm = PROJ_TM
    tiles_per_seq = seq // tm
    return pl.pallas_call(
        _in_proj_kernel,
        grid=(n_tok // tm, PROJ_COLS // PROJ_TN),
        in_specs=[
            pl.BlockSpec((tm, D_MODEL), lambda i, j: (i, 0)),
            pl.BlockSpec((D_MODEL, PROJ_TN), lambda i, j: (0, j)),
            pl.BlockSpec((1, PROJ_TN), lambda i, j: (0, j)),
            pl.BlockSpec((tm, HEAD_W), lambda i, j: (i % tiles_per_seq, 0)),
            pl.BlockSpec((tm, HEAD_W), lambda i, j: (i % tiles_per_seq, 0)),
            pl.BlockSpec((1, HEAD_W), lambda i, j: (0, 0)),
            pl.BlockSpec((1, HEAD_W), lambda i, j: (0, 0)),
        ],
        out_specs=pl.BlockSpec((tm, PROJ_TN), lambda i, j: (i, j)),
        out_shape=jax.ShapeDtypeStruct((n_tok, PROJ_COLS), BF16),
        compiler_params=_params(2),
        name="in_proj",
    )(x2d, w, col_scale, cos, sin, gq, gk)


def _bias_tiles_kernel(rb_ref, bucket_ref, o_ref):
    h = pl.program_id(0)
    bucket = bucket_ref[0]
    acc = jnp.zeros(bucket.shape, F32)
    for b in range(REL_BUCKETS):
        acc = jnp.where(bucket == b, rb_ref[b, h], acc)
    o_ref[0, 0] = acc * LOG2E


def _bias_tiles(rel_bias, bucket, t):
    return pl.pallas_call(
        _bias_tiles_kernel,
        grid=(A_HEADS, BIAS_DELTAS),
        in_specs=[
            pl.BlockSpec(memory_space=pltpu.SMEM),
            pl.BlockSpec((1, t, t), lambda h, d: (d, 0, 0)),
        ],
        out_specs=pl.BlockSpec((1, 1, t, t), lambda h, d: (h, d, 0, 0)),
        out_shape=jax.ShapeDtypeStruct((A_HEADS, BIAS_DELTAS, t, t), F32),
        compiler_params=_params(2),
        name="t5_bias_tiles",
    )(rel_bias, bucket)


def _online_softmax_loop(q, k_ref, v_ref, n_chunks, tk, score_fn, m_ref, acc_ref):
    m_ref[...] = jnp.full(m_ref.shape, -jnp.inf, F32)
    acc_ref[...] = jnp.zeros(acc_ref.shape, F32)

    n_rows = q.shape[0]
    lane_blocks = tk // HEAD_W
    ones = jnp.ones((tk, HEAD_W), BF16)

    def body(j, carry):
        start = pl.multiple_of(j * tk, tk)
        ks = k_ref[pl.ds(start, tk), :]
        v_ext = jnp.concatenate([v_ref[pl.ds(start, tk), :], ones], axis=1)
        s_all = lax.dot_general(q, ks, (((1,), (1,)), ((), ())), preferred_element_type=F32)
        for r0 in range(0, n_rows, ROW_BLOCK):
            rows = slice(r0, r0 + ROW_BLOCK)
            s = score_fn(s_all[rows], j, r0)
            m_prev = m_ref[rows, :]
            m_new = jnp.maximum(m_prev, jnp.max(s, axis=-1, keepdims=True))
            alpha = jnp.exp2(m_prev - m_new)
            p = jnp.exp2((s - jnp.tile(m_new, (1, lane_blocks))).astype(BF16))
            pv = jnp.dot(p, v_ext, preferred_element_type=F32)
            acc_ref[rows, :] = jnp.tile(alpha, (1, 2)) * acc_ref[rows, :] + pv
            m_ref[rows, :] = m_new
        return carry

    lax.fori_loop(0, n_chunks, body, 0, unroll=CHUNK_UNROLL)


def _attn_a_kernel(q_ref, k_ref, v_ref, bias_ref, lq1_ref, lk1_ref, lq2_ref, lk2_ref, g_ref,
                   o_ref, m_ref, acc_ref, *, n_chunks, lam_init):
    t = q_ref.shape[0]
    i = pl.program_id(2)
    q = q_ref[...]
    lane = lax.broadcasted_iota(jnp.int32, q.shape, 1)
    zero = jnp.zeros_like(q)
    q2m = jnp.concatenate([jnp.where(lane < A_QK_DIM, q, zero),
                           jnp.where(lane >= A_QK_DIM, q, zero)], axis=0)

    def score_fn(s, j, r0):
        d = jnp.clip(j - i, -(BIAS_DELTAS // 2), BIAS_DELTAS // 2) + BIAS_DELTAS // 2
        q0 = r0 % t
        return s + bias_ref[0, d, q0:q0 + ROW_BLOCK, :]

    _online_softmax_loop(q2m, k_ref, v_ref, n_chunks, t, score_fn, m_ref, acc_ref)

    lam = (jnp.exp(jnp.sum(lq1_ref[...] * lk1_ref[...], axis=-1, keepdims=True))
           - jnp.exp(jnp.sum(lq2_ref[...] * lk2_ref[...], axis=-1, keepdims=True))
           + lam_init)
    o1 = acc_ref[0:t, 0:HEAD_W] / acc_ref[0:t, HEAD_W:2 * HEAD_W]
    o2 = acc_ref[t:2 * t, 0:HEAD_W] / acc_ref[t:2 * t, HEAD_W:2 * HEAD_W]
    o = o1 - lam * o2
    ms = jnp.mean(o * o, axis=-1, keepdims=True)
    o = (o * lax.rsqrt(ms + RMS_EPS) * g_ref[...]) * (1.0 - lam_init)
    o_ref[...] = o.astype(o_ref.dtype)


def _attn_a(proj, bias, lq1, lk1, lq2, lk2, subln_g, batch, seq, lam_init):
    t = ATTN_A_T
    qt = seq // t
    lam_spec = pl.BlockSpec((1, A_QK_DIM), lambda b, h, i: (0, 0))
    return pl.pallas_call(
        functools.partial(_attn_a_kernel, n_chunks=seq // t, lam_init=lam_init),
        grid=(batch, A_HEADS, qt),
        in_specs=[
            pl.BlockSpec((t, HEAD_W), lambda b, h, i: (b * qt + i, h)),
            pl.BlockSpec((seq, HEAD_W), lambda b, h, i: (b, A_HEADS + h)),
            pl.BlockSpec((seq, HEAD_W), lambda b, h, i: (b, 2 * A_HEADS + h)),
            pl.BlockSpec((1, BIAS_DELTAS, t, t), lambda b, h, i: (h, 0, 0, 0)),
            lam_spec, lam_spec, lam_spec, lam_spec,
            pl.BlockSpec((1, HEAD_W), lambda b, h, i: (0, 0)),
        ],
        out_specs=pl.BlockSpec((t, HEAD_W), lambda b, h, i: (b * qt + i, h)),
        out_shape=jax.ShapeDtypeStruct((batch * seq, A_HEADS * HEAD_W), BF16),
        scratch_shapes=[
            pltpu.VMEM((2 * t, HEAD_W), F32),
            pltpu.VMEM((2 * t, 2 * HEAD_W), F32),
        ],
        compiler_params=_params(3),
        name="attn_a",
    )(proj, proj, proj, bias, lq1, lk1, lq2, lk2, subln_g)


def _attn_b_kernel(q_ref, k_ref, v_ref, o_ref, m_ref, acc_ref, *, n_chunks, tk):
    t = q_ref.shape[0]
    q = jnp.concatenate([q_ref[:, h * HEAD_W:(h + 1) * HEAD_W] for h in range(B_GROUP)], axis=0)

    _online_softmax_loop(q, k_ref, v_ref, n_chunks, tk, lambda s, j, r0: s,
                         m_ref, acc_ref)

    o = acc_ref[:, 0:HEAD_W] / acc_ref[:, HEAD_W:2 * HEAD_W]
    for h in range(B_GROUP):
        o_ref[:, h * HEAD_W:(h + 1) * HEAD_W] = o[h * t:(h + 1) * t, :].astype(o_ref.dtype)


def _attn_b(proj, batch, seq):
    t = ATTN_B_T
    tk = ATTN_B_TK
    qt = seq // t
    gw = B_GROUP * HEAD_W
    col0 = PROJ_A_COLS // HEAD_W
    return pl.pallas_call(
        functools.partial(_attn_b_kernel, n_chunks=seq // tk, tk=tk),
        grid=(batch, B_KV_HEADS, qt),
        in_specs=[
            pl.BlockSpec((t, gw), lambda b, g, i: (b * qt + i, PROJ_A_COLS // gw + g)),
            pl.BlockSpec((seq, HEAD_W), lambda b, g, i: (b, col0 + B_HEADS + g)),
            pl.BlockSpec((seq, HEAD_W), lambda b, g, i: (b, col0 + B_HEADS + B_KV_HEADS + g)),
        ],
        out_specs=pl.BlockSpec((t, gw), lambda b, g, i: (b * qt + i, g)),
        out_shape=jax.ShapeDtypeStruct((batch * seq, B_HEADS * HEAD_W), BF16),
        scratch_shapes=[
            pltpu.VMEM((B_GROUP * t, HEAD_W), F32),
            pltpu.VMEM((B_GROUP * t, 2 * HEAD_W), F32),
        ],
        compiler_params=_params(3),
        name="attn_b",
    )(proj, proj, proj)


def _layer_norm(x, g, b):
    mu = jnp.mean(x, axis=-1, keepdims=True)
    xc = x - mu
    var = jnp.mean(xc * xc, axis=-1, keepdims=True)
    return xc * lax.rsqrt(var + LN_EPS) * g + b


def _post_attn_kernel(x_ref, oa_ref, ob_ref, wga_ref, wgb_ref, wa_ref, wb_ref, wo_ref,
                      g_ref, b_ref, h_ref):
    x = x_ref[...]
    xb = x.astype(BF16)
    ga = jnp.dot(xb, wga_ref[...], preferred_element_type=F32)
    gb = jnp.dot(xb, wgb_ref[...], preferred_element_type=F32)
    ta = jnp.dot(oa_ref[...], wa_ref[...], preferred_element_type=F32)
    tb = jnp.dot(ob_ref[...], wb_ref[...], preferred_element_type=F32)
    merged = jax.nn.sigmoid(ga) * ta + jax.nn.sigmoid(gb) * tb
    mo = jnp.dot(merged.astype(BF16), wo_ref[...], preferred_element_type=F32)
    h_ref[...] = _layer_norm(DEEPNORM_ALPHA * x + mo, g_ref[...], b_ref[...])


def _post_attn(x2d, oa, ob, wga, wgb, wa, wb, wo, g, b):
    n_tok = x2d.shape[0]
    tm = POST_TM
    row = lambda i: (i, 0)
    const = lambda i: (0, 0)
    wspec = _resident((D_MODEL, D_MODEL), const)
    vspec = pl.BlockSpec((1, D_MODEL), const)
    return pl.pallas_call(
        _post_attn_kernel,
        grid=(n_tok // tm,),
        in_specs=[
            pl.BlockSpec((tm, D_MODEL), row),
            pl.BlockSpec((tm, D_MODEL), row),
            pl.BlockSpec((tm, D_MODEL), row),
            wspec, wspec, wspec, wspec, wspec, vspec, vspec,
        ],
        out_specs=pl.BlockSpec((tm, D_MODEL), row),
        out_shape=jax.ShapeDtypeStruct((n_tok, D_MODEL), F32),
        compiler_params=_params(1),
        name="post_attn",
    )(x2d, oa, ob, wga, wgb, wa, wb, wo, g, b)


def _ffn_kernel(h_ref, wg_ref, wu_ref, wd_ref, g_ref, b_ref, y_ref):
    h = h_ref[...]
    hb = h.astype(BF16)
    gate = jnp.dot(hb, wg_ref[...], preferred_element_type=F32)
    up = jnp.dot(hb, wu_ref[...], preferred_element_type=F32)
    act = jax.nn.silu(gate) * up
    f = jnp.dot(act.astype(BF16), wd_ref[...], preferred_element_type=F32)
    y_ref[...] = _layer_norm(DEEPNORM_ALPHA * h + f, g_ref[...], b_ref[...])


def _ffn(h2d, wg, wu, wd, g, b):
    n_tok = h2d.shape[0]
    d_ff = wg.shape[1]
    tm = FFN_TM
    row = lambda i: (i, 0)
    const = lambda i: (0, 0)
    vspec = pl.BlockSpec((1, D_MODEL), const)
    return pl.pallas_call(
        _ffn_kernel,
        grid=(n_tok // tm,),
        in_specs=[
            pl.BlockSpec((tm, D_MODEL), row),
            _resident((D_MODEL, d_ff), const),
            _resident((D_MODEL, d_ff), const),
            _resident((d_ff, D_MODEL), const),
            vspec, vspec,
        ],
        out_specs=pl.BlockSpec((tm, D_MODEL), row),
        out_shape=jax.ShapeDtypeStruct((n_tok, D_MODEL), F32),
        compiler_params=_params(1),
        name="ffn",
    )(h2d, wg, wu, wd, g, b)


def _t5_bucket(rel):
    nb = REL_BUCKETS // 2
    max_exact = nb // 2
    ret = (rel > 0).astype(jnp.int32) * nb
    n = jnp.abs(rel)
    nf = jnp.maximum(n, 1).astype(F32)
    large = max_exact + (jnp.log(nf / max_exact) / math.log(REL_MAX_DIST / max_exact)
                         * (nb - max_exact)).astype(jnp.int32)
    large = jnp.minimum(large, nb - 1)
    return ret + jnp.where(n < max_exact, n, large)


def _bucket_tiles(t):
    r = jnp.arange(t, dtype=jnp.int32)[None, :, None]
    c = jnp.arange(t, dtype=jnp.int32)[None, None, :]
    d = (jnp.arange(BIAS_DELTAS, dtype=jnp.int32) - BIAS_DELTAS // 2)[:, None, None] * t
    return _t5_bucket(c - r + d)


def _rope_tables(seq):
    tok = jnp.arange(seq)
    row = (tok // GRID_W).astype(F32)
    col = (tok % GRID_W).astype(F32)
    freqs = ROPE_THETA ** (-jnp.arange(0, ROPE_AXIS_DIM, 2, dtype=F32) / ROPE_AXIS_DIM)
    ang_r = row[:, None] * freqs[None, :]
    ang_c = col[:, None] * freqs[None, :]
    cos = jnp.concatenate([jnp.cos(ang_r), jnp.cos(ang_r), jnp.cos(ang_c), jnp.cos(ang_c)], axis=-1)
    sin = jnp.concatenate([-jnp.sin(ang_r), jnp.sin(ang_r), -jnp.sin(ang_c), jnp.sin(ang_c)], axis=-1)
    return cos, sin


def kernel(x_prompt, x_sample, w_in, lambda_q1, lambda_k1, lambda_q2, lambda_k2, subln_g,
           q_norm_g, k_norm_g, rel_bias, w_proj_a, w_proj_b, w_o, ln1_g, ln1_b,
           w_gate, w_up, w_down, ln2_g, ln2_b):
    assert w_in.shape[0] == DEPTH == 1
    assert ATTN_A_T >= REL_MAX_DIST
    lam_init = 0.8 - 0.6 * math.exp(-0.3 * 0)

    w_main = w_in[0][:, :PROJ_COLS].astype(BF16)
    wga = w_in[0][:, PROJ_COLS:PROJ_COLS + D_MODEL].astype(BF16)
    wgb = w_in[0][:, PROJ_COLS + D_MODEL:].astype(BF16)
    wa = w_proj_a[0].astype(BF16)
    wb = w_proj_b[0].astype(BF16)
    wo = w_o[0].astype(BF16)
    wg = w_gate[0].astype(BF16)
    wu = w_up[0].astype(BF16)
    wd = w_down[0].astype(BF16)

    col_scale = jnp.concatenate([jnp.full((1, A_Q_COLS), A_Q_SCALE, F32),
                                 jnp.ones((1, PROJ_COLS - A_Q_COLS), F32)], axis=1)
    bias = _bias_tiles(rel_bias, _bucket_tiles(ATTN_A_T), ATTN_A_T)

    def run(x):
        batch, seq, _ = x.shape
        x2d = x.reshape(batch * seq, D_MODEL)
        cos, sin = _rope_tables(seq)
        proj = _in_proj(x2d, w_main, col_scale, cos, sin, q_norm_g, k_norm_g, seq)
        oa = _attn_a(proj, bias, lambda_q1, lambda_k1, lambda_q2, lambda_k2, subln_g,
                     batch, seq, lam_init)
        ob = _attn_b(proj, batch, seq)
        h = _post_attn(x2d, oa, ob, wga, wgb, wa, wb, wo, ln1_g, ln1_b)
        y = _ffn(h, wg, wu, wd, ln2_g, ln2_b)
        return y.reshape(batch, seq, D_MODEL)

    return (run(x_prompt), run(x_sample))
```

```python
import functools
import math

import jax
import jax.numpy as jnp
from jax import lax
from jax.experimental import pallas as pl
from jax.experimental.pallas import tpu as pltpu

F32 = jnp.float32
BF16 = jnp.bfloat16

D_MODEL = 1024
GRID_W = 64
HEAD_W = 128
A_HEADS = 8
A_QK_DIM = 64
B_HEADS = 8
B_KV_HEADS = 2
B_GROUP = B_HEADS // B_KV_HEADS
ROPE_THETA = 10000.0
ROPE_AXIS_DIM = HEAD_W // 2
REL_BUCKETS = 32
REL_MAX_DIST = 128
LN_EPS = 1e-5
RMS_EPS = 1e-6
DEPTH = 1
DEEPNORM_ALPHA = (2 * DEPTH) ** 0.25
LOG2E = math.log2(math.e)
A_Q_SCALE = A_QK_DIM ** -0.5 * LOG2E
B_Q_SCALE = HEAD_W ** -0.5 * LOG2E

A_Q_COLS = A_HEADS * HEAD_W
PROJ_A_COLS = 3 * A_Q_COLS
PROJ_B_COLS = (B_HEADS + 2 * B_KV_HEADS) * HEAD_W
PROJ_COLS = PROJ_A_COLS + PROJ_B_COLS
PROJ_TN = PROJ_B_COLS

VMEM_LIMIT_BYTES = 48 * 1024 * 1024

PROJ_TM = 1024
ATTN_A_T = 1024
ATTN_A_TK = 512
ATTN_B_T = 512
ATTN_B_TK = 1024
POST_TM = 512
FFN_TM = 512
ROW_BLOCK = 128
CHUNK_UNROLL = 4

BIAS_OFF_HI = (ROW_BLOCK + REL_MAX_DIST - 2) // ROW_BLOCK
BIAS_OFF_LO = -((ATTN_A_TK + REL_MAX_DIST - 2) // ROW_BLOCK)
BIAS_TILES = BIAS_OFF_HI - BIAS_OFF_LO + 3


def _params(n_axes):
    return pltpu.CompilerParams(dimension_semantics=("arbitrary",) * n_axes,
                                vmem_limit_bytes=VMEM_LIMIT_BYTES)


def _resident(shape, index_map):
    return pl.BlockSpec(shape, index_map, pipeline_mode=pl.Buffered(1))


def _in_proj_kernel(x_ref, w_ref, cs_ref, cos_ref, sin_ref, gq_ref, gk_ref, o_ref):
    j = pl.program_id(1)
    res = jnp.dot(x_ref[...].astype(BF16), w_ref[...], preferred_element_type=F32)

    @pl.when(j < 2)
    def _():
        o_ref[...] = (res * cs_ref[...]).astype(BF16)

    @pl.when(j == 2)
    def _():
        cos = cos_ref[...]
        sin = sin_ref[...]
        lane = lax.broadcasted_iota(jnp.int32, cos.shape, 1)
        first_half = (lane % ROPE_AXIS_DIM) < (ROPE_AXIS_DIM // 2)
        n_rope = B_HEADS + B_KV_HEADS
        for h in range(PROJ_B_COLS // HEAD_W):
            v = res[:, h * HEAD_W:(h + 1) * HEAD_W]
            if h < n_rope:
                g = gq_ref[...] if h < B_HEADS else gk_ref[...]
                ms = jnp.mean(v * v, axis=-1, keepdims=True)
                vn = v * lax.rsqrt(ms + RMS_EPS) * g
                rot = jnp.where(first_half,
                                pltpu.roll(vn, HEAD_W - ROPE_AXIS_DIM // 2, 1),
                                pltpu.roll(vn, ROPE_AXIS_DIM // 2, 1))
                v = vn * cos + rot * sin
                if h < B_HEADS:
                    v = v * B_Q_SCALE
            o_ref[:, h * HEAD_W:(h + 1) * HEAD_W] = v.astype(BF16)


def _in_proj(x2d, w, col_scale, cos, sin, gq, gk, seq):
    n_tok = x2d.shape[0]
    tm = PROJ_TM
    tiles_per_seq = seq // tm
    return pl.pallas_call(
        _in_proj_kernel,
        grid=(n_tok // tm, PROJ_COLS // PROJ_TN),
        in_specs=[
            pl.BlockSpec((tm, D_MODEL), lambda i, j: (i, 0)),
            pl.BlockSpec((D_MODEL, PROJ_TN), lambda i, j: (0, j)),
            pl.BlockSpec((1, PROJ_TN), lambda i, j: (0, j)),
            pl.BlockSpec((tm, HEAD_W), lambda i, j: (i % tiles_per_seq, 0)),
            pl.BlockSpec((tm, HEAD_W), lambda i, j: (i % tiles_per_seq, 0)),
            pl.BlockSpec((1, HEAD_W), lambda i, j: (0, 0)),
            pl.BlockSpec((1, HEAD_W), lambda i, j: (0, 0)),
        ],
        out_specs=pl.BlockSpec((tm, PROJ_TN), lambda i, j: (i, j)),
        out_shape=jax.ShapeDtypeStruct((n_tok, PROJ_COLS), BF16),
        compiler_params=_params(2),
        name="in_proj",
    )(x2d, w, col_scale, cos, sin, gq, gk)


def _bias_tiles_kernel(rb_ref, bucket_ref, o_ref):
    h = pl.program_id(0)
    bucket = bucket_ref[0]
    acc = jnp.zeros(bucket.shape, F32)
    for b in range(REL_BUCKETS):
        acc = jnp.where(bucket == b, rb_ref[b, h], acc)
    o_ref[0, 0] = acc * LOG2E


def _bias_tiles(rel_bias, bucket):
    n_tiles, rb, tk = bucket.shape
    return pl.pallas_call(
        _bias_tiles_kernel,
        grid=(A_HEADS, n_tiles),
        in_specs=[
            pl.BlockSpec(memory_space=pltpu.SMEM),
            pl.BlockSpec((1, rb, tk), lambda h, d: (d, 0, 0)),
        ],
        out_specs=pl.BlockSpec((1, 1, rb, tk), lambda h, d: (h, d, 0, 0)),
        out_shape=jax.ShapeDtypeStruct((A_HEADS, n_tiles, rb, tk), F32),
        compiler_params=_params(2),
        name="t5_bias_tiles",
    )(rel_bias, bucket)


def _online_softmax_loop(q, k_ref, v_ref, n_chunks, tk, score_fn, m_ref, acc_ref):
    m_ref[...] = jnp.full(m_ref.shape, -jnp.inf, F32)
    acc_ref[...] = jnp.zeros(acc_ref.shape, F32)

    n_rows = q.shape[0]
    lane_blocks = tk // HEAD_W
    ones = jnp.ones((tk, HEAD_W), BF16)

    def body(j, carry):
        start = pl.multiple_of(j * tk, tk)
        ks = k_ref[pl.ds(start, tk), :]
        v_ext = jnp.concatenate([v_ref[pl.ds(start, tk), :], ones], axis=1)
        s_all = lax.dot_general(q, ks, (((1,), (1,)), ((), ())), preferred_element_type=F32)
        for r0 in range(0, n_rows, ROW_BLOCK):
            rows = slice(r0, r0 + ROW_BLOCK)
            s = score_fn(s_all[rows], j, r0)
            m_prev = m_ref[rows, :]
            m_new = jnp.maximum(m_prev, jnp.max(s, axis=-1, keepdims=True))
            alpha = jnp.exp2(m_prev - m_new)
            p = jnp.exp2((s - jnp.tile(m_new, (1, lane_blocks))).astype(BF16))
            pv = jnp.dot(p, v_ext, preferred_element_type=F32)
            acc_ref[rows, :] = jnp.tile(alpha, (1, 2)) * acc_ref[rows, :] + pv
            m_ref[rows, :] = m_new
        return carry

    lax.fori_loop(0, n_chunks, body, 0, unroll=min(CHUNK_UNROLL, n_chunks))


def _attn_a_kernel(q_ref, k_ref, v_ref, bias_ref, lq1_ref, lk1_ref, lq2_ref, lk2_ref, g_ref,
                   o_ref, m_ref, acc_ref, *, n_chunks, tk, lam_init):
    t = q_ref.shape[0]
    i = pl.program_id(2)
    q = q_ref[...]
    lane = lax.broadcasted_iota(jnp.int32, q.shape, 1)
    zero = jnp.zeros_like(q)
    q2m = jnp.concatenate([jnp.where(lane < A_QK_DIM, q, zero),
                           jnp.where(lane >= A_QK_DIM, q, zero)], axis=0)

    def score_fn(s, j, r0):
        off = j * (tk // ROW_BLOCK) - (i * (t // ROW_BLOCK) + (r0 % t) // ROW_BLOCK)
        d = jnp.clip(off, BIAS_OFF_LO - 1, BIAS_OFF_HI + 1) - (BIAS_OFF_LO - 1)
        return s + bias_ref[0, d]

    _online_softmax_loop(q2m, k_ref, v_ref, n_chunks, tk, score_fn, m_ref, acc_ref)

    lam = (jnp.exp(jnp.sum(lq1_ref[...] * lk1_ref[...], axis=-1, keepdims=True))
           - jnp.exp(jnp.sum(lq2_ref[...] * lk2_ref[...], axis=-1, keepdims=True))
           + lam_init)
    o1 = acc_ref[0:t, 0:HEAD_W] / acc_ref[0:t, HEAD_W:2 * HEAD_W]
    o2 = acc_ref[t:2 * t, 0:HEAD_W] / acc_ref[t:2 * t, HEAD_W:2 * HEAD_W]
    o = o1 - lam * o2
    ms = jnp.mean(o * o, axis=-1, keepdims=True)
    o = (o * lax.rsqrt(ms + RMS_EPS) * g_ref[...]) * (1.0 - lam_init)
    o_ref[...] = o.astype(o_ref.dtype)


def _attn_a(proj, bias, lq1, lk1, lq2, lk2, subln_g, batch, seq, lam_init):
    t = ATTN_A_T
    tk = ATTN_A_TK
    qt = seq // t
    lam_spec = pl.BlockSpec((1, A_QK_DIM), lambda b, h, i: (0, 0))
    return pl.pallas_call(
        functools.partial(_attn_a_kernel, n_chunks=seq // tk, tk=tk, lam_init=lam_init),
        grid=(batch, A_HEADS, qt),
        in_specs=[
            pl.BlockSpec((t, HEAD_W), lambda b, h, i: (b * qt + i, h)),
            pl.BlockSpec((seq, HEAD_W), lambda b, h, i: (b, A_HEADS + h)),
            pl.BlockSpec((seq, HEAD_W), lambda b, h, i: (b, 2 * A_HEADS + h)),
            pl.BlockSpec((1, BIAS_TILES, ROW_BLOCK, tk), lambda b, h, i: (h, 0, 0, 0)),
            lam_spec, lam_spec, lam_spec, lam_spec,
            pl.BlockSpec((1, HEAD_W), lambda b, h, i: (0, 0)),
        ],
        out_specs=pl.BlockSpec((t, HEAD_W), lambda b, h, i: (b * qt + i, h)),
        out_shape=jax.ShapeDtypeStruct((batch * seq, A_HEADS * HEAD_W), BF16),
        scratch_shapes=[
            pltpu.VMEM((2 * t, HEAD_W), F32),
            pltpu.VMEM((2 * t, 2 * HEAD_W), F32),
        ],
        compiler_params=_params(3),
        name="attn_a",
    )(proj, proj, proj, bias, lq1, lk1, lq2, lk2, subln_g)


def _attn_b_kernel(q_ref, k_ref, v_ref, o_ref, m_ref, acc_ref, *, n_chunks, tk):
    t = q_ref.shape[0]
    q = jnp.concatenate([q_ref[:, h * HEAD_W:(h + 1) * HEAD_W] for h in range(B_GROUP)], axis=0)

    _online_softmax_loop(q, k_ref, v_ref, n_chunks, tk, lambda s, j, r0: s,
                         m_ref, acc_ref)

    o = acc_ref[:, 0:HEAD_W] / acc_ref[:, HEAD_W:2 * HEAD_W]
    for h in range(B_GROUP):
        o_ref[:, h * HEAD_W:(h + 1) * HEAD_W] = o[h * t:(h + 1) * t, :].astype(o_ref.dtype)


def _attn_b(proj, batch, seq):
    t = ATTN_B_T
    tk = ATTN_B_TK
    qt = seq // t
    gw = B_GROUP * HEAD_W
    col0 = PROJ_A_COLS // HEAD_W
    return pl.pallas_call(
        functools.partial(_attn_b_kernel, n_chunks=seq // tk, tk=tk),
        grid=(batch, B_KV_HEADS, qt),
        in_specs=[
            pl.BlockSpec((t, gw), lambda b, g, i: (b * qt + i, PROJ_A_COLS // gw + g)),
            pl.BlockSpec((seq, HEAD_W), lambda b, g, i: (b, col0 + B_HEADS + g)),
            pl.BlockSpec((seq, HEAD_W), lambda b, g, i: (b, col0 + B_HEADS + B_KV_HEADS + g)),
        ],
        out_specs=pl.BlockSpec((t, gw), lambda b, g, i: (b * qt + i, g)),
        out_shape=jax.ShapeDtypeStruct((batch * seq, B_HEADS * HEAD_W), BF16),
        scratch_shapes=[
            pltpu.VMEM((B_GROUP * t, HEAD_W), F32),
            pltpu.VMEM((B_GROUP * t, 2 * HEAD_W), F32),
        ],
        compiler_params=_params(3),
        name="attn_b",
    )(proj, proj, proj)


def _layer_norm(x, g, b):
    mu = jnp.mean(x, axis=-1, keepdims=True)
    xc = x - mu
    var = jnp.mean(xc * xc, axis=-1, keepdims=True)
    return xc * lax.rsqrt(var + LN_EPS) * g + b


def _post_attn_kernel(x_ref, oa_ref, ob_ref, wga_ref, wgb_ref, wa_ref, wb_ref, wo_ref,
                      g_ref, b_ref, h_ref):
    x = x_ref[...]
    xb = x.astype(BF16)
    ga = jnp.dot(xb, wga_ref[...], preferred_element_type=F32)
    gb = jnp.dot(xb, wgb_ref[...], preferred_element_type=F32)
    ta = jnp.dot(oa_ref[...], wa_ref[...], preferred_element_type=F32)
    tb = jnp.dot(ob_ref[...], wb_ref[...], preferred_element_type=F32)
    merged = jax.nn.sigmoid(ga) * ta + jax.nn.sigmoid(gb) * tb
    mo = jnp.dot(merged.astype(BF16), wo_ref[...], preferred_element_type=F32)
    h_ref[...] = _layer_norm(DEEPNORM_ALPHA * x + mo, g_ref[...], b_ref[...])


def _post_attn(x2d, oa, ob, wga, wgb, wa, wb, wo, g, b):
    n_tok = x2d.shape[0]
    tm = POST_TM
    row = lambda i: (i, 0)
    const = lambda i: (0, 0)
    wspec = _resident((D_MODEL, D_MODEL), const)
    vspec = pl.BlockSpec((1, D_MODEL), const)
    return pl.pallas_call(
        _post_attn_kernel,
        grid=(n_tok // tm,),
        in_specs=[
            pl.BlockSpec((tm, D_MODEL), row),
            pl.BlockSpec((tm, D_MODEL), row),
            pl.BlockSpec((tm, D_MODEL), row),
            wspec, wspec, wspec, wspec, wspec, vspec, vspec,
        ],
        out_specs=pl.BlockSpec((tm, D_MODEL), row),
        out_shape=jax.ShapeDtypeStruct((n_tok, D_MODEL), F32),
        compiler_params=_params(1),
        name="post_attn",
    )(x2d, oa, ob, wga, wgb, wa, wb, wo, g, b)


def _ffn_kernel(h_ref, wg_ref, wu_ref, wd_ref, g_ref, b_ref, y_ref):
    h = h_ref[...]
    hb = h.astype(BF16)
    gate = jnp.dot(hb, wg_ref[...], preferred_element_type=F32)
    up = jnp.dot(hb, wu_ref[...], preferred_element_type=F32)
    act = jax.nn.silu(gate) * up
    f = jnp.dot(act.astype(BF16), wd_ref[...], preferred_element_type=F32)
    y_ref[...] = _layer_norm(DEEPNORM_ALPHA * h + f, g_ref[...], b_ref[...])


def _ffn(h2d, wg, wu, wd, g, b):
    n_tok = h2d.shape[0]
    d_ff = wg.shape[1]
    tm = FFN_TM
    row = lambda i: (i, 0)
    const = lambda i: (0, 0)
    vspec = pl.BlockSpec((1, D_MODEL), const)
    return pl.pallas_call(
        _ffn_kernel,
        grid=(n_tok // tm,),
        in_specs=[
            pl.BlockSpec((tm, D_MODEL), row),
            _resident((D_MODEL, d_ff), const),
            _resident((D_MODEL, d_ff), const),
            _resident((d_ff, D_MODEL), const),
            vspec, vspec,
        ],
        out_specs=pl.BlockSpec((tm, D_MODEL), row),
        out_shape=jax.ShapeDtypeStruct((n_tok, D_MODEL), F32),
        compiler_params=_params(1),
        name="ffn",
    )(h2d, wg, wu, wd, g, b)


def _t5_bucket(rel):
    nb = REL_BUCKETS // 2
    max_exact = nb // 2
    ret = (rel > 0).astype(jnp.int32) * nb
    n = jnp.abs(rel)
    nf = jnp.maximum(n, 1).astype(F32)
    large = max_exact + (jnp.log(nf / max_exact) / math.log(REL_MAX_DIST / max_exact)
                         * (nb - max_exact)).astype(jnp.int32)
    large = jnp.minimum(large, nb - 1)
    return ret + jnp.where(n < max_exact, n, large)


def _bucket_tiles():
    r = jnp.arange(ROW_BLOCK, dtype=jnp.int32)[None, :, None]
    c = jnp.arange(ATTN_A_TK, dtype=jnp.int32)[None, None, :]
    off = (jnp.arange(BIAS_TILES, dtype=jnp.int32) + (BIAS_OFF_LO - 1))[:, None, None] * ROW_BLOCK
    return _t5_bucket(c - r + off)


def _rope_tables(seq):
    tok = jnp.arange(seq)
    row = (tok // GRID_W).astype(F32)
    col = (tok % GRID_W).astype(F32)
    freqs = ROPE_THETA ** (-jnp.arange(0, ROPE_AXIS_DIM, 2, dtype=F32) / ROPE_AXIS_DIM)
    ang_r = row[:, None] * freqs[None, :]
    ang_c = col[:, None] * freqs[None, :]
    cos = jnp.concatenate([jnp.cos(ang_r), jnp.cos(ang_r), jnp.cos(ang_c), jnp.cos(ang_c)], axis=-1)
    sin = jnp.concatenate([-jnp.sin(ang_r), jnp.sin(ang_r), -jnp.sin(ang_c), jnp.sin(ang_c)], axis=-1)
    return cos, sin


def kernel(x_prompt, x_sample, w_in, lambda_q1, lambda_k1, lambda_q2, lambda_k2, subln_g,
           q_norm_g, k_norm_g, rel_bias, w_proj_a, w_proj_b, w_o, ln1_g, ln1_b,
           w_gate, w_up, w_down, ln2_g, ln2_b):
    assert w_in.shape[0] == DEPTH == 1
    lam_init = 0.8 - 0.6 * math.exp(-0.3 * 0)

    w_main = w_in[0][:, :PROJ_COLS].astype(BF16)
    wga = w_in[0][:, PROJ_COLS:PROJ_COLS + D_MODEL].astype(BF16)
    wgb = w_in[0][:, PROJ_COLS + D_MODEL:].astype(BF16)
    wa = w_proj_a[0].astype(BF16)
    wb = w_proj_b[0].astype(BF16)
    wo = w_o[0].astype(BF16)
    wg = w_gate[0].astype(BF16)
    wu = w_up[0].astype(BF16)
    wd = w_down[0].astype(BF16)

    col_scale = jnp.concatenate([jnp.full((1, A_Q_COLS), A_Q_SCALE, F32),
                                 jnp.ones((1, PROJ_COLS - A_Q_COLS), F32)], axis=1)
    bias = _bias_tiles(rel_bias, _bucket_tiles())

    def run(x):
        batch, seq, _ = x.shape
        x2d = x.reshape(batch * seq, D_MODEL)
        cos, sin = _rope_tables(seq)
        proj = _in_proj(x2d, w_main, col_scale, cos, sin, q_norm_g, k_norm_g, seq)
        oa = _attn_a(proj, bias, lambda_q1, lambda_k1, lambda_q2, lambda_k2, subln_g,
                     batch, seq, lam_init)
        ob = _attn_b(proj, batch, seq)
        h = _post_attn(x2d, oa, ob, wga, wgb, wa, wb, wo, ln1_g, ln1_b)
        y = _ffn(h, wg, wu, wd, ln2_g, ln2_b)
        return y.reshape(batch, seq, D_MODEL)

    return (run(x_prompt), run(x_sample))
```

```python
import functools
import math

import jax
import jax.numpy as jnp
from jax import lax
from jax.experimental import pallas as pl
from jax.experimental.pallas import tpu as pltpu

F32 = jnp.float32
BF16 = jnp.bfloat16

D_MODEL = 1024
GRID_W = 64
HEAD_W = 128
A_HEADS = 8
A_QK_DIM = 64
B_HEADS = 8
B_KV_HEADS = 2
B_GROUP = B_HEADS // B_KV_HEADS
ROPE_THETA = 10000.0
ROPE_AXIS_DIM = HEAD_W // 2
REL_BUCKETS = 32
REL_MAX_DIST = 128
LN_EPS = 1e-5
RMS_EPS = 1e-6
DEPTH = 1
DEEPNORM_ALPHA = (2 * DEPTH) ** 0.25
LOG2E = math.log2(math.e)
A_Q_SCALE = A_QK_DIM ** -0.5 * LOG2E
B_Q_SCALE = HEAD_W ** -0.5 * LOG2E

A_Q_COLS = A_HEADS * HEAD_W
PROJ_A_COLS = 3 * A_Q_COLS
PROJ_B_COLS = (B_HEADS + 2 * B_KV_HEADS) * HEAD_W
PROJ_COLS = PROJ_A_COLS + PROJ_B_COLS
PROJ_TN = PROJ_B_COLS

VMEM_LIMIT_BYTES = 48 * 1024 * 1024

PROJ_TM = 1024
ATTN_A_T = 1024
ATTN_A_TK = 512
ATTN_B_T = 512
ATTN_B_TK = 512
POST_TM = 512
FFN_TM = 512
ROW_BLOCK = 128
MAX_CHUNK_UNROLL = 5

BIAS_OFF_HI = (ROW_BLOCK + REL_MAX_DIST - 2) // ROW_BLOCK
BIAS_OFF_LO = -((ATTN_A_TK + REL_MAX_DIST - 2) // ROW_BLOCK)
BIAS_TILES = BIAS_OFF_HI - BIAS_OFF_LO + 3


def _params(n_axes):
    return pltpu.CompilerParams(dimension_semantics=("arbitrary",) * n_axes,
                                vmem_limit_bytes=VMEM_LIMIT_BYTES)


def _resident(shape, index_map):
    return pl.BlockSpec(shape, index_map, pipeline_mode=pl.Buffered(1))


def _in_proj_kernel(x_ref, w_ref, cs_ref, cos_ref, sin_ref, gq_ref, gk_ref, oa_ref, ob_ref, b_ref):
    j = pl.program_id(1)
    xb = x_ref[...].astype(BF16)

    def matmul():
        return jnp.dot(xb, w_ref[...], preferred_element_type=F32)

    @pl.when(j == 0)
    def _():
        b_ref[...] = matmul()

    @pl.when(j == 2)
    def _():
        oa_ref[...] = (matmul() * cs_ref[...]).astype(BF16)

    @pl.when(j == 1)
    def _():
        oa_ref[...] = (matmul() * cs_ref[...]).astype(BF16)
        cos = cos_ref[...]
        sin = sin_ref[...]
        lane = lax.broadcasted_iota(jnp.int32, cos.shape, 1)
        first_half = (lane % ROPE_AXIS_DIM) < (ROPE_AXIS_DIM // 2)
        n_rope = B_HEADS + B_KV_HEADS
        for h in range(PROJ_B_COLS // HEAD_W):
            v = b_ref[:, h * HEAD_W:(h + 1) * HEAD_W]
            if h < n_rope:
                g = gq_ref[...] if h < B_HEADS else gk_ref[...]
                ms = jnp.mean(v * v, axis=-1, keepdims=True)
                vn = v * lax.rsqrt(ms + RMS_EPS) * g
                rot = jnp.where(first_half,
                                pltpu.roll(vn, HEAD_W - ROPE_AXIS_DIM // 2, 1),
                                pltpu.roll(vn, ROPE_AXIS_DIM // 2, 1))
                v = vn * cos + rot * sin
                if h < B_HEADS:
                    v = v * B_Q_SCALE
            ob_ref[:, h * HEAD_W:(h + 1) * HEAD_W] = v.astype(BF16)


def _in_proj(x2d, w, col_scale, cos, sin, gq, gk, seq):
    n_tok = x2d.shape[0]
    tm = PROJ_TM
    tiles_per_seq = seq // tm
    n_col_tiles = PROJ_COLS // PROJ_TN
    w_tile = lambda i, j: (0, (j + n_col_tiles - 1) % n_col_tiles)
    return pl.pallas_call(
        _in_proj_kernel,
        grid=(n_tok // tm, n_col_tiles),
        in_specs=[
            pl.BlockSpec((tm, D_MODEL), lambda i, j: (i, 0)),
            pl.BlockSpec((D_MODEL, PROJ_TN), w_tile),
            pl.BlockSpec((1, PROJ_TN), w_tile),
            pl.BlockSpec((tm, HEAD_W), lambda i, j: (i % tiles_per_seq, 0)),
            pl.BlockSpec((tm, HEAD_W), lambda i, j: (i % tiles_per_seq, 0)),
            pl.BlockSpec((1, HEAD_W), lambda i, j: (0, 0)),
            pl.BlockSpec((1, HEAD_W), lambda i, j: (0, 0)),
        ],
        out_specs=[
            pl.BlockSpec((tm, PROJ_TN), lambda i, j: (i, jnp.maximum(j - 1, 0))),
            pl.BlockSpec((tm, PROJ_TN), lambda i, j: (i, 0)),
        ],
        out_shape=[jax.ShapeDtypeStruct((n_tok, PROJ_A_COLS), BF16),
                   jax.ShapeDtypeStruct((n_tok, PROJ_B_COLS), BF16)],
        scratch_shapes=[pltpu.VMEM((tm, PROJ_TN), F32)],
        compiler_params=_params(2),
        name="in_proj",
    )(x2d, w, col_scale, cos, sin, gq, gk)


def _bias_tiles_kernel(rb_ref, bucket_ref, o_ref):
    h = pl.program_id(0)
    bucket = bucket_ref[0]
    acc = jnp.zeros(bucket.shape, F32)
    for b in range(REL_BUCKETS):
        acc = jnp.where(bucket == b, rb_ref[b, h], acc)
    o_ref[0, 0] = acc * LOG2E


def _bias_tiles(rel_bias, bucket):
    n_tiles, rb, tk = bucket.shape
    return pl.pallas_call(
        _bias_tiles_kernel,
        grid=(A_HEADS, n_tiles),
        in_specs=[
            pl.BlockSpec(memory_space=pltpu.SMEM),
            pl.BlockSpec((1, rb, tk), lambda h, d: (d, 0, 0)),
        ],
        out_specs=pl.BlockSpec((1, 1, rb, tk), lambda h, d: (h, d, 0, 0)),
        out_shape=jax.ShapeDtypeStruct((A_HEADS, n_tiles, rb, tk), F32),
        compiler_params=_params(2),
        name="t5_bias_tiles",
    )(rel_bias, bucket)


def _online_softmax_loop(q, k_ref, v_ref, n_chunks, tk, score_fn, m_ref, acc_ref):
    n_rows = q.shape[0]
    lane_blocks = tk // HEAD_W
    ones = jnp.ones((tk, HEAD_W), BF16)

    def chunk(j, first):
        start = pl.multiple_of(j * tk, tk)
        ks = k_ref[pl.ds(start, tk), :]
        v_ext = jnp.concatenate([v_ref[pl.ds(start, tk), :], ones], axis=1)
        s_all = lax.dot_general(q, ks, (((1,), (1,)), ((), ())), preferred_element_type=F32)
        for r0 in range(0, n_rows, ROW_BLOCK):
            rows = slice(r0, r0 + ROW_BLOCK)
            s = score_fn(s_all[rows], j, r0)
            m_cur = jnp.max(s, axis=-1, keepdims=True)
            if first:
                m_new = jnp.broadcast_to(m_cur, (ROW_BLOCK, HEAD_W))
            else:
                m_prev = m_ref[rows, :]
                m_new = jnp.maximum(m_prev, m_cur)
            p = jnp.exp2((s - jnp.tile(m_new, (1, lane_blocks))).astype(BF16))
            pv = jnp.dot(p, v_ext, preferred_element_type=F32)
            if first:
                acc_ref[rows, :] = pv
            else:
                alpha = jnp.exp2(m_prev - m_new)
                acc_ref[rows, :] = jnp.tile(alpha, (1, 2)) * acc_ref[rows, :] + pv
            m_ref[rows, :] = m_new

    chunk(0, True)
    rest = n_chunks - 1
    unroll = max(u for u in range(1, MAX_CHUNK_UNROLL + 1) if rest % u == 0)

    def body(j, carry):
        chunk(j, False)
        return carry

    lax.fori_loop(1, n_chunks, body, 0, unroll=unroll)


def _attn_a_kernel(q_ref, k_ref, v_ref, bias_ref, lq1_ref, lk1_ref, lq2_ref, lk2_ref, g_ref,
                   o_ref, m_ref, acc_ref, *, n_chunks, tk, lam_init):
    t = q_ref.shape[0]
    i = pl.program_id(2)
    q = q_ref[...]
    lane = lax.broadcasted_iota(jnp.int32, q.shape, 1)
    zero = jnp.zeros_like(q)
    q2m = jnp.concatenate([jnp.where(lane < A_QK_DIM, q, zero),
                           jnp.where(lane >= A_QK_DIM, q, zero)], axis=0)

    def score_fn(s, j, r0):
        off = j * (tk // ROW_BLOCK) - (i * (t // ROW_BLOCK) + (r0 % t) // ROW_BLOCK)
        d = jnp.clip(off, BIAS_OFF_LO - 1, BIAS_OFF_HI + 1) - (BIAS_OFF_LO - 1)
        return s + bias_ref[0, d]

    _online_softmax_loop(q2m, k_ref, v_ref, n_chunks, tk, score_fn, m_ref, acc_ref)

    lam = (jnp.exp(jnp.sum(lq1_ref[...] * lk1_ref[...], axis=-1, keepdims=True))
           - jnp.exp(jnp.sum(lq2_ref[...] * lk2_ref[...], axis=-1, keepdims=True))
           + lam_init)
    o1 = acc_ref[0:t, 0:HEAD_W] / acc_ref[0:t, HEAD_W:2 * HEAD_W]
    o2 = acc_ref[t:2 * t, 0:HEAD_W] / acc_ref[t:2 * t, HEAD_W:2 * HEAD_W]
    o = o1 - lam * o2
    ms = jnp.mean(o * o, axis=-1, keepdims=True)
    o = (o * lax.rsqrt(ms + RMS_EPS) * g_ref[...]) * (1.0 - lam_init)
    o_ref[...] = o.astype(o_ref.dtype)


def _attn_a(proj, bias, lq1, lk1, lq2, lk2, subln_g, batch, seq, lam_init):
    t = ATTN_A_T
    tk = ATTN_A_TK
    qt = seq // t
    lam_spec = pl.BlockSpec((1, A_QK_DIM), lambda b, h, i: (0, 0))
    return pl.pallas_call(
        functools.partial(_attn_a_kernel, n_chunks=seq // tk, tk=tk, lam_init=lam_init),
        grid=(batch, A_HEADS, qt),
        in_specs=[
            pl.BlockSpec((t, HEAD_W), lambda b, h, i: (b * qt + i, h)),
            pl.BlockSpec((seq, HEAD_W), lambda b, h, i: (b, A_HEADS + h)),
            pl.BlockSpec((seq, HEAD_W), lambda b, h, i: (b, 2 * A_HEADS + h)),
            pl.BlockSpec((1, BIAS_TILES, ROW_BLOCK, tk), lambda b, h, i: (h, 0, 0, 0)),
            lam_spec, lam_spec, lam_spec, lam_spec,
            pl.BlockSpec((1, HEAD_W), lambda b, h, i: (0, 0)),
        ],
        out_specs=pl.BlockSpec((t, HEAD_W), lambda b, h, i: (b * qt + i, h)),
        out_shape=jax.ShapeDtypeStruct((batch * seq, A_HEADS * HEAD_W), BF16),
        scratch_shapes=[
            pltpu.VMEM((2 * t, HEAD_W), F32),
            pltpu.VMEM((2 * t, 2 * HEAD_W), F32),
        ],
        compiler_params=_params(3),
        name="attn_a",
    )(proj, proj, proj, bias, lq1, lk1, lq2, lk2, subln_g)


def _attn_b_kernel(q_ref, k_ref, v_ref, o_ref, m_ref, acc_ref, *, n_chunks, tk):
    t = q_ref.shape[0]
    q = jnp.concatenate([q_ref[:, h * HEAD_W:(h + 1) * HEAD_W] for h in range(B_GROUP)], axis=0)

    _online_softmax_loop(q, k_ref, v_ref, n_chunks, tk, lambda s, j, r0: s,
                         m_ref, acc_ref)

    o = acc_ref[:, 0:HEAD_W] / acc_ref[:, HEAD_W:2 * HEAD_W]
    for h in range(B_GROUP):
        o_ref[:, h * HEAD_W:(h + 1) * HEAD_W] = o[h * t:(h + 1) * t, :].astype(o_ref.dtype)


def _attn_b(proj, batch, seq):
    t = ATTN_B_T
    tk = ATTN_B_TK
    qt = seq // t
    gw = B_GROUP * HEAD_W
    return pl.pallas_call(
        functools.partial(_attn_b_kernel, n_chunks=seq // tk, tk=tk),
        grid=(batch, B_KV_HEADS, qt),
        in_specs=[
            pl.BlockSpec((t, gw), lambda b, g, i: (b * qt + i, g)),
            pl.BlockSpec((seq, HEAD_W), lambda b, g, i: (b, B_HEADS + g)),
            pl.BlockSpec((seq, HEAD_W), lambda b, g, i: (b, B_HEADS + B_KV_HEADS + g)),
        ],
        out_specs=pl.BlockSpec((t, gw), lambda b, g, i: (b * qt + i, g)),
        out_shape=jax.ShapeDtypeStruct((batch * seq, B_HEADS * HEAD_W), BF16),
        scratch_shapes=[
            pltpu.VMEM((B_GROUP * t, HEAD_W), F32),
            pltpu.VMEM((B_GROUP * t, 2 * HEAD_W), F32),
        ],
        compiler_params=_params(3),
        name="attn_b",
    )(proj, proj, proj)


def _layer_norm(x, g, b):
    mu = jnp.mean(x, axis=-1, keepdims=True)
    xc = x - mu
    var = jnp.mean(xc * xc, axis=-1, keepdims=True)
    return xc * lax.rsqrt(var + LN_EPS) * g + b


def _post_attn_kernel(x_ref, oa_ref, ob_ref, wga_ref, wgb_ref, wa_ref, wb_ref, wo_ref,
                      g_ref, b_ref, h_ref):
    x = x_ref[...]
    xb = x.astype(BF16)
    ga = jnp.dot(xb, wga_ref[...], preferred_element_type=F32)
    gb = jnp.dot(xb, wgb_ref[...], preferred_element_type=F32)
    ta = jnp.dot(oa_ref[...], wa_ref[...], preferred_element_type=F32)
    tb = jnp.dot(ob_ref[...], wb_ref[...], preferred_element_type=F32)
    merged = jax.nn.sigmoid(ga) * ta + jax.nn.sigmoid(gb) * tb
    mo = jnp.dot(merged.astype(BF16), wo_ref[...], preferred_element_type=F32)
    h_ref[...] = _layer_norm(DEEPNORM_ALPHA * x + mo, g_ref[...], b_ref[...])


def _post_attn(x2d, oa, ob, wga, wgb, wa, wb, wo, g, b):
    n_tok = x2d.shape[0]
    tm = POST_TM
    row = lambda i: (i, 0)
    const = lambda i: (0, 0)
    wspec = _resident((D_MODEL, D_MODEL), const)
    vspec = pl.BlockSpec((1, D_MODEL), const)
    return pl.pallas_call(
        _post_attn_kernel,
        grid=(n_tok // tm,),
        in_specs=[
            pl.BlockSpec((tm, D_MODEL), row),
            pl.BlockSpec((tm, D_MODEL), row),
            pl.BlockSpec((tm, D_MODEL), row),
            wspec, wspec, wspec, wspec, wspec, vspec, vspec,
        ],
        out_specs=pl.BlockSpec((tm, D_MODEL), row),
        out_shape=jax.ShapeDtypeStruct((n_tok, D_MODEL), F32),
        compiler_params=_params(1),
        name="post_attn",
    )(x2d, oa, ob, wga, wgb, wa, wb, wo, g, b)


def _ffn_kernel(h_ref, wg_ref, wu_ref, wd_ref, g_ref, b_ref, y_ref):
    h = h_ref[...]
    hb = h.astype(BF16)
    gate = jnp.dot(hb, wg_ref[...], preferred_element_type=F32)
    up = jnp.dot(hb, wu_ref[...], preferred_element_type=F32)
    act = jax.nn.silu(gate) * up
    f = jnp.dot(act.astype(BF16), wd_ref[...], preferred_element_type=F32)
    y_ref[...] = _layer_norm(DEEPNORM_ALPHA * h + f, g_ref[...], b_ref[...])


def _ffn(h2d, wg, wu, wd, g, b):
    n_tok = h2d.shape[0]
    d_ff = wg.shape[1]
    tm = FFN_TM
    row = lambda i: (i, 0)
    const = lambda i: (0, 0)
    vspec = pl.BlockSpec((1, D_MODEL), const)
    return pl.pallas_call(
        _ffn_kernel,
        grid=(n_tok // tm,),
        in_specs=[
            pl.BlockSpec((tm, D_MODEL), row),
            _resident((D_MODEL, d_ff), const),
            _resident((D_MODEL, d_ff), const),
            _resident((d_ff, D_MODEL), const),
            vspec, vspec,
        ],
        out_specs=pl.BlockSpec((tm, D_MODEL), row),
        out_shape=jax.ShapeDtypeStruct((n_tok, D_MODEL), F32),
        compiler_params=_params(1),
        name="ffn",
    )(h2d, wg, wu, wd, g, b)


def _t5_bucket(rel):
    nb = REL_BUCKETS // 2
    max_exact = nb // 2
    ret = (rel > 0).astype(jnp.int32) * nb
    n = jnp.abs(rel)
    nf = jnp.maximum(n, 1).astype(F32)
    large = max_exact + (jnp.log(nf / max_exact) / math.log(REL_MAX_DIST / max_exact)
                         * (nb - max_exact)).astype(jnp.int32)
    large = jnp.minimum(large, nb - 1)
    return ret + jnp.where(n < max_exact, n, large)


def _bucket_tiles():
    r = jnp.arange(ROW_BLOCK, dtype=jnp.int32)[None, :, None]
    c = jnp.arange(ATTN_A_TK, dtype=jnp.int32)[None, None, :]
    off = (jnp.arange(BIAS_TILES, dtype=jnp.int32) + (BIAS_OFF_LO - 1))[:, None, None] * ROW_BLOCK
    return _t5_bucket(c - r + off)


def _rope_tables(seq):
    tok = jnp.arange(seq)
    row = (tok // GRID_W).astype(F32)
    col = (tok % GRID_W).astype(F32)
    freqs = ROPE_THETA ** (-jnp.arange(0, ROPE_AXIS_DIM, 2, dtype=F32) / ROPE_AXIS_DIM)
    ang_r = row[:, None] * freqs[None, :]
    ang_c = col[:, None] * freqs[None, :]
    cos = jnp.concatenate([jnp.cos(ang_r), jnp.cos(ang_r), jnp.cos(ang_c), jnp.cos(ang_c)], axis=-1)
    sin = jnp.concatenate([-jnp.sin(ang_r), jnp.sin(ang_r), -jnp.sin(ang_c), jnp.sin(ang_c)], axis=-1)
    return cos, sin


def kernel(x_prompt, x_sample, w_in, lambda_q1, lambda_k1, lambda_q2, lambda_k2, subln_g,
           q_norm_g, k_norm_g, rel_bias, w_proj_a, w_proj_b, w_o, ln1_g, ln1_b,
           w_gate, w_up, w_down, ln2_g, ln2_b):
    assert w_in.shape[0] == DEPTH == 1
    lam_init = 0.8 - 0.6 * math.exp(-0.3 * 0)

    w_main = w_in[0][:, :PROJ_COLS].astype(BF16)
    wga = w_in[0][:, PROJ_COLS:PROJ_COLS + D_MODEL].astype(BF16)
    wgb = w_in[0][:, PROJ_COLS + D_MODEL:].astype(BF16)
    wa = w_proj_a[0].astype(BF16)
    wb = w_proj_b[0].astype(BF16)
    wo = w_o[0].astype(BF16)
    wg = w_gate[0].astype(BF16)
    wu = w_up[0].astype(BF16)
    wd = w_down[0].astype(BF16)

    col_scale = jnp.concatenate([jnp.full((1, A_Q_COLS), A_Q_SCALE, F32),
                                 jnp.ones((1, PROJ_COLS - A_Q_COLS), F32)], axis=1)
    bias = _bias_tiles(rel_bias, _bucket_tiles())

    def run(x):
        batch, seq, _ = x.shape
        x2d = x.reshape(batch * seq, D_MODEL)
        cos, sin = _rope_tables(seq)
        proj_a, proj_b = _in_proj(x2d, w_main, col_scale, cos, sin, q_norm_g, k_norm_g, seq)
        oa = _attn_a(proj_a, bias, lambda_q1, lambda_k1, lambda_q2, lambda_k2, subln_g,
                     batch, seq, lam_init)
        ob = _attn_b(proj_b, batch, seq)
        h = _post_attn(x2d, oa, ob, wga, wgb, wa, wb, wo, ln1_g, ln1_b)
        y = _ffn(h, wg, wu, wd, ln2_g, ln2_b)
        return y.reshape(batch, seq, D_MODEL)

    return (run(x_prompt), run(x_sample))
```

```python
import functools
import math

import jax
import jax.numpy as jnp
import numpy as np
from jax import lax
from jax.experimental import pallas as pl
from jax.experimental.pallas import tpu as pltpu

F32 = jnp.float32
BF16 = jnp.bfloat16

D_MODEL = 1024
GRID_W = 64
HEAD_W = 128
A_HEADS = 8
A_QK_DIM = 64
B_HEADS = 8
B_KV_HEADS = 2
B_GROUP = B_HEADS // B_KV_HEADS
ROPE_THETA = 10000.0
ROPE_AXIS_DIM = HEAD_W // 2
REL_BUCKETS = 32
REL_MAX_DIST = 128
LN_EPS = 1e-5
RMS_EPS = 1e-6
DEPTH = 1
DEEPNORM_ALPHA = (2 * DEPTH) ** 0.25
LOG2E = math.log2(math.e)
A_Q_SCALE = A_QK_DIM ** -0.5 * LOG2E
B_Q_SCALE = HEAD_W ** -0.5 * LOG2E
ROPE_PERM = np.array([q * (ROPE_AXIS_DIM // 2) + i for q in (0, 2, 1, 3) for i in range(ROPE_AXIS_DIM // 2)],
                     dtype=np.int32)

A_Q_COLS = A_HEADS * HEAD_W
PROJ_A_COLS = 3 * A_Q_COLS
PROJ_B_COLS = (B_HEADS + 2 * B_KV_HEADS) * HEAD_W
PROJ_COLS = PROJ_A_COLS + PROJ_B_COLS
PROJ_TN = PROJ_B_COLS

VMEM_LIMIT_BYTES = 48 * 1024 * 1024

PROJ_TM = 1024
ATTN_A_T = 2048
ATTN_A_TK = 512
ATTN_B_T = 1024
ATTN_B_TK = 512
POST_TM = 512
FFN_TM = 512
ROW_BLOCK = 128
MAX_CHUNK_UNROLL = 5

BIAS_OFF_HI = (ROW_BLOCK + REL_MAX_DIST - 2) // ROW_BLOCK
BIAS_OFF_LO = -((ATTN_A_TK + REL_MAX_DIST - 2) // ROW_BLOCK)
BIAS_TILES = BIAS_OFF_HI - BIAS_OFF_LO + 3


def _params(n_axes):
    return pltpu.CompilerParams(dimension_semantics=("arbitrary",) * n_axes,
                                vmem_limit_bytes=VMEM_LIMIT_BYTES)


def _resident(shape, index_map):
    return pl.BlockSpec(shape, index_map, pipeline_mode=pl.Buffered(1))


def _in_proj_kernel(x_ref, w_ref, cs_ref, cos_ref, sin_ref, gq_ref, gk_ref, oa_ref, ob_ref, b_ref):
    j = pl.program_id(1)
    xb = x_ref[...].astype(BF16)

    def matmul():
        return jnp.dot(xb, w_ref[...], preferred_element_type=F32)

    @pl.when(j == 0)
    def _():
        b_ref[...] = matmul()

    @pl.when(j == 2)
    def _():
        oa_ref[...] = (matmul() * cs_ref[...]).astype(BF16)

    @pl.when(j == 1)
    def _():
        oa_ref[...] = (matmul() * cs_ref[...]).astype(BF16)
        cos = cos_ref[...]
        sin = sin_ref[...]
        n_rope = B_HEADS + B_KV_HEADS
        for h in range(PROJ_B_COLS // HEAD_W):
            v = b_ref[:, h * HEAD_W:(h + 1) * HEAD_W]
            if h < n_rope:
                g = gq_ref[...] if h < B_HEADS else gk_ref[...]
                ms = jnp.mean(v * v, axis=-1, keepdims=True)
                vn = v * lax.rsqrt(ms + RMS_EPS) * g
                v = vn * cos + pltpu.roll(vn, HEAD_W // 2, 1) * sin
                if h < B_HEADS:
                    v = v * B_Q_SCALE
            ob_ref[:, h * HEAD_W:(h + 1) * HEAD_W] = v.astype(BF16)


def _in_proj(x2d, w, col_scale, cos, sin, gq, gk, seq):
    n_tok = x2d.shape[0]
    tm = PROJ_TM
    tiles_per_seq = seq // tm
    n_col_tiles = PROJ_COLS // PROJ_TN
    w_tile = lambda i, j: (0, (j + n_col_tiles - 1) % n_col_tiles)
    return pl.pallas_call(
        _in_proj_kernel,
        grid=(n_tok // tm, n_col_tiles),
        in_specs=[
            pl.BlockSpec((tm, D_MODEL), lambda i, j: (i, 0)),
            pl.BlockSpec((D_MODEL, PROJ_TN), w_tile),
            pl.BlockSpec((1, PROJ_TN), w_tile),
            pl.BlockSpec((tm, HEAD_W), lambda i, j: (i % tiles_per_seq, 0)),
            pl.BlockSpec((tm, HEAD_W), lambda i, j: (i % tiles_per_seq, 0)),
            pl.BlockSpec((1, HEAD_W), lambda i, j: (0, 0)),
            pl.BlockSpec((1, HEAD_W), lambda i, j: (0, 0)),
        ],
        out_specs=[
            pl.BlockSpec((tm, PROJ_TN), lambda i, j: (i, jnp.maximum(j - 1, 0))),
            pl.BlockSpec((tm, PROJ_TN), lambda i, j: (i, 0)),
        ],
        out_shape=[jax.ShapeDtypeStruct((n_tok, PROJ_A_COLS), BF16),
                   jax.ShapeDtypeStruct((n_tok, PROJ_B_COLS), BF16)],
        scratch_shapes=[pltpu.VMEM((tm, PROJ_TN), F32)],
        compiler_params=_params(2),
        name="in_proj",
    )(x2d, w, col_scale, cos, sin, gq, gk)


def _bias_tiles_kernel(rb_ref, bucket_ref, o_ref):
    h = pl.program_id(0)
    bucket = bucket_ref[0]
    acc = jnp.zeros(bucket.shape, F32)
    for b in range(REL_BUCKETS):
        acc = jnp.where(bucket == b, rb_ref[b, h], acc)
    o_ref[0, 0] = acc * LOG2E


def _bias_tiles(rel_bias, bucket):
    n_tiles, rb, tk = bucket.shape
    return pl.pallas_call(
        _bias_tiles_kernel,
        grid=(A_HEADS, n_tiles),
        in_specs=[
            pl.BlockSpec(memory_space=pltpu.SMEM),
            pl.BlockSpec((1, rb, tk), lambda h, d: (d, 0, 0)),
        ],
        out_specs=pl.BlockSpec((1, 1, rb, tk), lambda h, d: (h, d, 0, 0)),
        out_shape=jax.ShapeDtypeStruct((A_HEADS, n_tiles, rb, tk), F32),
        compiler_params=_params(2),
        name="t5_bias_tiles",
    )(rel_bias, bucket)


def _online_softmax_loop(q, k_ref, v_ref, n_chunks, tk, score_fn, m_ref, acc_ref):
    n_rows = q.shape[0]
    lane_blocks = tk // HEAD_W
    ones = jnp.ones((tk, HEAD_W), BF16)

    def chunk(j, first):
        start = pl.multiple_of(j * tk, tk)
        ks = k_ref[pl.ds(start, tk), :]
        v_ext = jnp.concatenate([v_ref[pl.ds(start, tk), :], ones], axis=1)
        s_all = lax.dot_general(q, ks, (((1,), (1,)), ((), ())), preferred_element_type=F32)
        for r0 in range(0, n_rows, ROW_BLOCK):
            rows = slice(r0, r0 + ROW_BLOCK)
            s = score_fn(s_all[rows], j, r0)
            m_cur = jnp.max(s, axis=-1, keepdims=True)
            if first:
                m_new = jnp.broadcast_to(m_cur, (ROW_BLOCK, HEAD_W))
            else:
                m_prev = m_ref[rows, :]
                m_new = jnp.maximum(m_prev, m_cur)
            p = jnp.exp2((s - jnp.tile(m_new, (1, lane_blocks))).astype(BF16))
            pv = jnp.dot(p, v_ext, preferred_element_type=F32)
            if first:
                acc_ref[rows, :] = pv
            else:
                alpha = jnp.exp2(m_prev - m_new)
                acc_ref[rows, :] = jnp.tile(alpha, (1, 2)) * acc_ref[rows, :] + pv
            m_ref[rows, :] = m_new

    chunk(0, True)
    rest = n_chunks - 1
    unroll = max(u for u in range(1, MAX_CHUNK_UNROLL + 1) if rest % u == 0)

    def body(j, carry):
        chunk(j, False)
        return carry

    lax.fori_loop(1, n_chunks, body, 0, unroll=unroll)


def _softmax_scratch(n_rows):
    return [
        pltpu.VMEM((n_rows, HEAD_W), F32),
        pltpu.VMEM((n_rows, 2 * HEAD_W), F32),
    ]


def _attn_a_kernel(q_ref, k_ref, v_ref, bias_ref, lq1_ref, lk1_ref, lq2_ref, lk2_ref, g_ref,
                   o_ref, m_ref, acc_ref, *, n_chunks, tk, lam_init):
    t = q_ref.shape[0]
    i = pl.program_id(2)
    q = q_ref[...]
    lane = lax.broadcasted_iota(jnp.int32, q.shape, 1)
    zero = jnp.zeros_like(q)
    q2m = jnp.concatenate([jnp.where(lane < A_QK_DIM, q, zero),
                           jnp.where(lane >= A_QK_DIM, q, zero)], axis=0)

    def score_fn(s, j, r0):
        off = j * (tk // ROW_BLOCK) - (i * (t // ROW_BLOCK) + (r0 % t) // ROW_BLOCK)
        d = jnp.clip(off, BIAS_OFF_LO - 1, BIAS_OFF_HI + 1) - (BIAS_OFF_LO - 1)
        return s + bias_ref[0, d]

    _online_softmax_loop(q2m, k_ref, v_ref, n_chunks, tk, score_fn, m_ref, acc_ref)

    lam = (jnp.exp(jnp.sum(lq1_ref[...] * lk1_ref[...], axis=-1, keepdims=True))
           - jnp.exp(jnp.sum(lq2_ref[...] * lk2_ref[...], axis=-1, keepdims=True))
           + lam_init)
    o1 = acc_ref[0:t, 0:HEAD_W] / acc_ref[0:t, HEAD_W:2 * HEAD_W]
    o2 = acc_ref[t:2 * t, 0:HEAD_W] / acc_ref[t:2 * t, HEAD_W:2 * HEAD_W]
    o = o1 - lam * o2
    ms = jnp.mean(o * o, axis=-1, keepdims=True)
    o = (o * lax.rsqrt(ms + RMS_EPS) * g_ref[...]) * (1.0 - lam_init)
    o_ref[...] = o.astype(o_ref.dtype)


def _attn_a(proj, bias, lq1, lk1, lq2, lk2, subln_g, batch, seq, lam_init):
    t = ATTN_A_T
    tk = ATTN_A_TK
    qt = seq // t
    lam_spec = pl.BlockSpec((1, A_QK_DIM), lambda b, h, i: (0, 0))
    return pl.pallas_call(
        functools.partial(_attn_a_kernel, n_chunks=seq // tk, tk=tk, lam_init=lam_init),
        grid=(batch, A_HEADS, qt),
        in_specs=[
            pl.BlockSpec((t, HEAD_W), lambda b, h, i: (b * qt + i, h)),
            pl.BlockSpec((seq, HEAD_W), lambda b, h, i: (b, A_HEADS + h)),
            pl.BlockSpec((seq, HEAD_W), lambda b, h, i: (b, 2 * A_HEADS + h)),
            pl.BlockSpec((1, BIAS_TILES, ROW_BLOCK, tk), lambda b, h, i: (h, 0, 0, 0)),
            lam_spec, lam_spec, lam_spec, lam_spec,
            pl.BlockSpec((1, HEAD_W), lambda b, h, i: (0, 0)),
        ],
        out_specs=pl.BlockSpec((t, HEAD_W), lambda b, h, i: (b * qt + i, h)),
        out_shape=jax.ShapeDtypeStruct((batch * seq, A_HEADS * HEAD_W), BF16),
        scratch_shapes=_softmax_scratch(2 * t),
        compiler_params=_params(3),
        name="attn_a",
    )(proj, proj, proj, bias, lq1, lk1, lq2, lk2, subln_g)


def _attn_b_kernel(q_ref, k_ref, v_ref, o_ref, m_ref, acc_ref, *, n_chunks, tk):
    t = q_ref.shape[0]
    q = jnp.concatenate([q_ref[:, h * HEAD_W:(h + 1) * HEAD_W] for h in range(B_GROUP)], axis=0)

    _online_softmax_loop(q, k_ref, v_ref, n_chunks, tk, lambda s, j, r0: s,
                         m_ref, acc_ref)

    o = acc_ref[:, 0:HEAD_W] / acc_ref[:, HEAD_W:2 * HEAD_W]
    for h in range(B_GROUP):
        o_ref[:, h * HEAD_W:(h + 1) * HEAD_W] = o[h * t:(h + 1) * t, :].astype(o_ref.dtype)


def _attn_b(proj, batch, seq):
    t = ATTN_B_T
    tk = ATTN_B_TK
    qt = seq // t
    gw = B_GROUP * HEAD_W
    return pl.pallas_call(
        functools.partial(_attn_b_kernel, n_chunks=seq // tk, tk=tk),
        grid=(batch, B_KV_HEADS, qt),
        in_specs=[
            pl.BlockSpec((t, gw), lambda b, g, i: (b * qt + i, g)),
            pl.BlockSpec((seq, HEAD_W), lambda b, g, i: (b, B_HEADS + g)),
            pl.BlockSpec((seq, HEAD_W), lambda b, g, i: (b, B_HEADS + B_KV_HEADS + g)),
        ],
        out_specs=pl.BlockSpec((t, gw), lambda b, g, i: (b * qt + i, g)),
        out_shape=jax.ShapeDtypeStruct((batch * seq, B_HEADS * HEAD_W), BF16),
        scratch_shapes=_softmax_scratch(B_GROUP * t),
        compiler_params=_params(3),
        name="attn_b",
    )(proj, proj, proj)


def _layer_norm(x, g, b):
    mu = jnp.mean(x, axis=-1, keepdims=True)
    xc = x - mu
    var = jnp.mean(xc * xc, axis=-1, keepdims=True)
    return xc * lax.rsqrt(var + LN_EPS) * g + b


def _post_attn_kernel(x_ref, oa_ref, ob_ref, wga_ref, wgb_ref, wa_ref, wb_ref, wo_ref,
                      g_ref, b_ref, h_ref):
    x = x_ref[...]
    xb = x.astype(BF16)
    ga = jnp.dot(xb, wga_ref[...], preferred_element_type=F32)
    gb = jnp.dot(xb, wgb_ref[...], preferred_element_type=F32)
    ta = jnp.dot(oa_ref[...], wa_ref[...], preferred_element_type=F32)
    tb = jnp.dot(ob_ref[...], wb_ref[...], preferred_element_type=F32)
    merged = jax.nn.sigmoid(ga) * ta + jax.nn.sigmoid(gb) * tb
    mo = jnp.dot(merged.astype(BF16), wo_ref[...], preferred_element_type=F32)
    h_ref[...] = _layer_norm(DEEPNORM_ALPHA * x + mo, g_ref[...], b_ref[...])


def _post_attn(x2d, oa, ob, wga, wgb, wa, wb, wo, g, b):
    n_tok = x2d.shape[0]
    tm = POST_TM
    row = lambda i: (i, 0)
    const = lambda i: (0, 0)
    wspec = _resident((D_MODEL, D_MODEL), const)
    vspec = pl.BlockSpec((1, D_MODEL), const)
    return pl.pallas_call(
        _post_attn_kernel,
        grid=(n_tok // tm,),
        in_specs=[
            pl.BlockSpec((tm, D_MODEL), row),
            pl.BlockSpec((tm, D_MODEL), row),
            pl.BlockSpec((tm, D_MODEL), row),
            wspec, wspec, wspec, wspec, wspec, vspec, vspec,
        ],
        out_specs=pl.BlockSpec((tm, D_MODEL), row),
        out_shape=jax.ShapeDtypeStruct((n_tok, D_MODEL), F32),
        compiler_params=_params(1),
        name="post_attn",
    )(x2d, oa, ob, wga, wgb, wa, wb, wo, g, b)


def _ffn_kernel(h_ref, wg_ref, wu_ref, wd_ref, g_ref, b_ref, y_ref):
    h = h_ref[...]
    hb = h.astype(BF16)
    gate = jnp.dot(hb, wg_ref[...], preferred_element_type=F32)
    up = jnp.dot(hb, wu_ref[...], preferred_element_type=F32)
    act = jax.nn.silu(gate) * up
    f = jnp.dot(act.astype(BF16), wd_ref[...], preferred_element_type=F32)
    y_ref[...] = _layer_norm(DEEPNORM_ALPHA * h + f, g_ref[...], b_ref[...])


def _ffn(h2d, wg, wu, wd, g, b):
    n_tok = h2d.shape[0]
    d_ff = wg.shape[1]
    tm = FFN_TM
    row = lambda i: (i, 0)
    const = lambda i: (0, 0)
    vspec = pl.BlockSpec((1, D_MODEL), const)
    return pl.pallas_call(
        _ffn_kernel,
        grid=(n_tok // tm,),
        in_specs=[
            pl.BlockSpec((tm, D_MODEL), row),
            _resident((D_MODEL, d_ff), const),
            _resident((D_MODEL, d_ff), const),
            _resident((d_ff, D_MODEL), const),
            vspec, vspec,
        ],
        out_specs=pl.BlockSpec((tm, D_MODEL), row),
        out_shape=jax.ShapeDtypeStruct((n_tok, D_MODEL), F32),
        compiler_params=_params(1),
        name="ffn",
    )(h2d, wg, wu, wd, g, b)


def _t5_bucket(rel):
    nb = REL_BUCKETS // 2
    max_exact = nb // 2
    ret = (rel > 0).astype(jnp.int32) * nb
    n = jnp.abs(rel)
    nf = jnp.maximum(n, 1).astype(F32)
    large = max_exact + (jnp.log(nf / max_exact) / math.log(REL_MAX_DIST / max_exact)
                         * (nb - max_exact)).astype(jnp.int32)
    large = jnp.minimum(large, nb - 1)
    return ret + jnp.where(n < max_exact, n, large)


def _bucket_tiles():
    r = jnp.arange(ROW_BLOCK, dtype=jnp.int32)[None, :, None]
    c = jnp.arange(ATTN_A_TK, dtype=jnp.int32)[None, None, :]
    off = (jnp.arange(BIAS_TILES, dtype=jnp.int32) + (BIAS_OFF_LO - 1))[:, None, None] * ROW_BLOCK
    return _t5_bucket(c - r + off)


def _rope_tables(seq):
    tok = jnp.arange(seq)
    row = (tok // GRID_W).astype(F32)
    col = (tok % GRID_W).astype(F32)
    freqs = ROPE_THETA ** (-jnp.arange(0, ROPE_AXIS_DIM, 2, dtype=F32) / ROPE_AXIS_DIM)
    ang_r = row[:, None] * freqs[None, :]
    ang_c = col[:, None] * freqs[None, :]
    cos = jnp.concatenate([jnp.cos(ang_r), jnp.cos(ang_r), jnp.cos(ang_c), jnp.cos(ang_c)], axis=-1)
    sin = jnp.concatenate([-jnp.sin(ang_r), jnp.sin(ang_r), -jnp.sin(ang_c), jnp.sin(ang_c)], axis=-1)
    return cos[:, ROPE_PERM], sin[:, ROPE_PERM]


def kernel(x_prompt, x_sample, w_in, lambda_q1, lambda_k1, lambda_q2, lambda_k2, subln_g,
           q_norm_g, k_norm_g, rel_bias, w_proj_a, w_proj_b, w_o, ln1_g, ln1_b,
           w_gate, w_up, w_down, ln2_g, ln2_b):
    assert w_in.shape[0] == DEPTH == 1
    lam_init = 0.8 - 0.6 * math.exp(-0.3 * 0)

    n_rope_cols = (B_HEADS + B_KV_HEADS) * HEAD_W
    w_rope = w_in[0][:, PROJ_A_COLS:PROJ_A_COLS + n_rope_cols].reshape(D_MODEL, -1, HEAD_W)
    w_main = jnp.concatenate([w_in[0][:, :PROJ_A_COLS],
                              w_rope[:, :, ROPE_PERM].reshape(D_MODEL, n_rope_cols),
                              w_in[0][:, PROJ_A_COLS + n_rope_cols:PROJ_COLS]], axis=1).astype(BF16)
    q_norm_g = q_norm_g[:, ROPE_PERM]
    k_norm_g = k_norm_g[:, ROPE_PERM]
    wga = w_in[0][:, PROJ_COLS:PROJ_COLS + D_MODEL].astype(BF16)
    wgb = w_in[0][:, PROJ_COLS + D_MODEL:].astype(BF16)
    wa = w_proj_a[0].astype(BF16)
    wb = w_proj_b[0].astype(BF16)
    wo = w_o[0].astype(BF16)
    wg = w_gate[0].astype(BF16)
    wu = w_up[0].astype(BF16)
    wd = w_down[0].astype(BF16)

    col_scale = jnp.concatenate([jnp.full((1, A_Q_COLS), A_Q_SCALE, F32),
                                 jnp.ones((1, PROJ_COLS - A_Q_COLS), F32)], axis=1)
    bias = _bias_tiles(rel_bias, _bucket_tiles())

    def run(x):
        batch, seq, _ = x.shape
        x2d = x.reshape(batch * seq, D_MODEL)
        cos, sin = _rope_tables(seq)
        proj_a, proj_b = _in_proj(x2d, w_main, col_scale, cos, sin, q_norm_g, k_norm_g, seq)
        oa = _attn_a(proj_a, bias, lambda_q1, lambda_k1, lambda_q2, lambda_k2, subln_g,
                     batch, seq, lam_init)
        ob = _attn_b(proj_b, batch, seq)
        h = _post_attn(x2d, oa, ob, wga, wgb, wa, wb, wo, ln1_g, ln1_b)
        y = _ffn(h, wg, wu, wd, ln2_g, ln2_b)
        return y.reshape(batch, seq, D_MODEL)

    return (run(x_prompt), run(x_sample))
```

```python
import functools
import math

import jax
import jax.numpy as jnp
import numpy as np
from jax import lax
from jax.experimental import pallas as pl
from jax.experimental.pallas import tpu as pltpu

F32 = jnp.float32
BF16 = jnp.bfloat16

D_MODEL = 1024
GRID_W = 64
HEAD_W = 128
A_HEADS = 8
A_QK_DIM = 64
B_HEADS = 8
B_KV_HEADS = 2
B_GROUP = B_HEADS // B_KV_HEADS
ROPE_THETA = 10000.0
ROPE_AXIS_DIM = HEAD_W // 2
REL_BUCKETS = 32
REL_MAX_DIST = 128
LN_EPS = 1e-5
RMS_EPS = 1e-6
DEPTH = 1
DEEPNORM_ALPHA = (2 * DEPTH) ** 0.25
LOG2E = math.log2(math.e)
A_Q_SCALE = A_QK_DIM ** -0.5 * LOG2E
B_Q_SCALE = HEAD_W ** -0.5 * LOG2E
ROPE_PERM = np.array([q * (ROPE_AXIS_DIM // 2) + i for q in (0, 2, 1, 3) for i in range(ROPE_AXIS_DIM // 2)],
                     dtype=np.int32)

A_Q_COLS = A_HEADS * HEAD_W
PROJ_A_COLS = 3 * A_Q_COLS
PROJ_B_COLS = (B_HEADS + 2 * B_KV_HEADS) * HEAD_W
PROJ_COLS = PROJ_A_COLS + PROJ_B_COLS
PROJ_TN = PROJ_B_COLS

VMEM_LIMIT_BYTES = 48 * 1024 * 1024

PROJ_TM = 512
ATTN_A_T = 2048
ATTN_A_TK = 512
ATTN_B_T = 1024
ATTN_B_TK = 512
POST_TM = 512
FFN_TM = 512
ROW_BLOCK = 128
MAX_CHUNK_UNROLL = 3

BIAS_OFF_HI = (ROW_BLOCK + REL_MAX_DIST - 2) // ROW_BLOCK
BIAS_OFF_LO = -((ATTN_A_TK + REL_MAX_DIST - 2) // ROW_BLOCK)
BIAS_TILES = BIAS_OFF_HI - BIAS_OFF_LO + 3


def _params(n_axes):
    return pltpu.CompilerParams(dimension_semantics=("arbitrary",) * n_axes,
                                vmem_limit_bytes=VMEM_LIMIT_BYTES)


def _resident(shape, index_map):
    return pl.BlockSpec(shape, index_map, pipeline_mode=pl.Buffered(1))


def _in_proj_kernel(x_ref, w_ref, cs_ref, cos_ref, sin_ref, gq_ref, gk_ref, oa_ref, ob_ref):
    xb = x_ref[...].astype(BF16)

    def matmul(c0, c1):
        return jnp.dot(xb, w_ref[:, c0:c1], preferred_element_type=F32)

    res_b = matmul(PROJ_A_COLS, PROJ_COLS)
    for c0 in range(0, PROJ_A_COLS, PROJ_TN):
        c1 = c0 + PROJ_TN
        oa_ref[:, c0:c1] = (matmul(c0, c1) * cs_ref[:, c0:c1]).astype(BF16)

    cos = cos_ref[...]
    sin = sin_ref[...]
    n_rope = B_HEADS + B_KV_HEADS
    for h in range(PROJ_B_COLS // HEAD_W):
        v = res_b[:, h * HEAD_W:(h + 1) * HEAD_W]
        if h < n_rope:
            g = gq_ref[...] if h < B_HEADS else gk_ref[...]
            ms = jnp.mean(v * v, axis=-1, keepdims=True)
            vn = v * lax.rsqrt(ms + RMS_EPS) * g
            v = vn * cos + pltpu.roll(vn, HEAD_W // 2, 1) * sin
            if h < B_HEADS:
                v = v * B_Q_SCALE
        ob_ref[:, h * HEAD_W:(h + 1) * HEAD_W] = v.astype(BF16)


def _in_proj(x2d, w, col_scale, cos, sin, gq, gk, seq):
    n_tok = x2d.shape[0]
    tm = PROJ_TM
    tiles_per_seq = seq // tm
    row = lambda i: (i, 0)
    const = lambda i: (0, 0)
    return pl.pallas_call(
        _in_proj_kernel,
        grid=(n_tok // tm,),
        in_specs=[
            pl.BlockSpec((tm, D_MODEL), row),
            _resident((D_MODEL, PROJ_COLS), const),
            pl.BlockSpec((1, PROJ_A_COLS), const),
            pl.BlockSpec((tm, HEAD_W), lambda i: (i % tiles_per_seq, 0)),
            pl.BlockSpec((tm, HEAD_W), lambda i: (i % tiles_per_seq, 0)),
            pl.BlockSpec((1, HEAD_W), const),
            pl.BlockSpec((1, HEAD_W), const),
        ],
        out_specs=[pl.BlockSpec((tm, PROJ_A_COLS), row), pl.BlockSpec((tm, PROJ_B_COLS), row)],
        out_shape=[jax.ShapeDtypeStruct((n_tok, PROJ_A_COLS), BF16),
                   jax.ShapeDtypeStruct((n_tok, PROJ_B_COLS), BF16)],
        compiler_params=_params(1),
        name="in_proj",
    )(x2d, w, col_scale, cos, sin, gq, gk)


def _bias_tiles_kernel(rb_ref, bucket_ref, o_ref):
    h = pl.program_id(0)
    bucket = bucket_ref[0]
    acc = jnp.zeros(bucket.shape, F32)
    for b in range(REL_BUCKETS):
        acc = jnp.where(bucket == b, rb_ref[b, h], acc)
    o_ref[0, 0] = acc * LOG2E


def _bias_tiles(rel_bias, bucket):
    n_tiles, rb, tk = bucket.shape
    return pl.pallas_call(
        _bias_tiles_kernel,
        grid=(A_HEADS, n_tiles),
        in_specs=[
            pl.BlockSpec(memory_space=pltpu.SMEM),
            pl.BlockSpec((1, rb, tk), lambda h, d: (d, 0, 0)),
        ],
        out_specs=pl.BlockSpec((1, 1, rb, tk), lambda h, d: (h, d, 0, 0)),
        out_shape=jax.ShapeDtypeStruct((A_HEADS, n_tiles, rb, tk), F32),
        compiler_params=_params(2),
        name="t5_bias_tiles",
    )(rel_bias, bucket)


def _online_softmax_loop(q, k_ref, v_ref, n_chunks, tk, score_fn, m_ref, acc_ref):
    n_rows = q.shape[0]
    lane_blocks = tk // HEAD_W
    ones = jnp.ones((tk, HEAD_W), BF16)

    def chunk(j, first):
        start = pl.multiple_of(j * tk, tk)
        ks = k_ref[pl.ds(start, tk), :]
        v_ext = jnp.concatenate([v_ref[pl.ds(start, tk), :], ones], axis=1)
        s_all = lax.dot_general(q, ks, (((1,), (1,)), ((), ())), preferred_element_type=F32)
        for r0 in range(0, n_rows, ROW_BLOCK):
            rows = slice(r0, r0 + ROW_BLOCK)
            s = score_fn(s_all[rows], j, r0)
            m_cur = jnp.max(s, axis=-1, keepdims=True)
            if first:
                m_new = jnp.broadcast_to(m_cur, (ROW_BLOCK, HEAD_W))
            else:
                m_prev = m_ref[rows, :]
                m_new = jnp.maximum(m_prev, m_cur)
            p = jnp.exp2((s - jnp.tile(m_new, (1, lane_blocks))).astype(BF16))
            pv = jnp.dot(p, v_ext, preferred_element_type=F32)
            if first:
                acc_ref[rows, :] = pv
            else:
                alpha = jnp.exp2(m_prev - m_new)
                acc_ref[rows, :] = jnp.tile(alpha, (1, 2)) * acc_ref[rows, :] + pv
            m_ref[rows, :] = m_new

    chunk(0, True)
    rest = n_chunks - 1
    unroll = max(u for u in range(1, MAX_CHUNK_UNROLL + 1) if rest % u == 0)

    def body(j, carry):
        chunk(j, False)
        return carry

    lax.fori_loop(1, n_chunks, body, 0, unroll=unroll)


def _q_tile(full_tile, seq, tk):
    return min(seq, full_tile if seq // tk - 1 <= MAX_CHUNK_UNROLL else full_tile // 2)


def _softmax_scratch(n_rows):
    return [
        pltpu.VMEM((n_rows, HEAD_W), F32),
        pltpu.VMEM((n_rows, 2 * HEAD_W), F32),
    ]


def _attn_a_kernel(q_ref, k_ref, v_ref, bias_ref, lq1_ref, lk1_ref, lq2_ref, lk2_ref, g_ref,
                   o_ref, m_ref, acc_ref, *, n_chunks, tk, lam_init):
    t = q_ref.shape[0]
    i = pl.program_id(2)
    q = q_ref[...]
    lane = lax.broadcasted_iota(jnp.int32, q.shape, 1)
    zero = jnp.zeros_like(q)
    q2m = jnp.concatenate([jnp.where(lane < A_QK_DIM, q, zero),
                           jnp.where(lane >= A_QK_DIM, q, zero)], axis=0)

    def score_fn(s, j, r0):
        off = j * (tk // ROW_BLOCK) - (i * (t // ROW_BLOCK) + (r0 % t) // ROW_BLOCK)
        d = jnp.clip(off, BIAS_OFF_LO - 1, BIAS_OFF_HI + 1) - (BIAS_OFF_LO - 1)
        return s + bias_ref[0, d]

    _online_softmax_loop(q2m, k_ref, v_ref, n_chunks, tk, score_fn, m_ref, acc_ref)

    lam = (jnp.exp(jnp.sum(lq1_ref[...] * lk1_ref[...], axis=-1, keepdims=True))
           - jnp.exp(jnp.sum(lq2_ref[...] * lk2_ref[...], axis=-1, keepdims=True))
           + lam_init)
    o1 = acc_ref[0:t, 0:HEAD_W] / acc_ref[0:t, HEAD_W:2 * HEAD_W]
    o2 = acc_ref[t:2 * t, 0:HEAD_W] / acc_ref[t:2 * t, HEAD_W:2 * HEAD_W]
    o = o1 - lam * o2
    ms = jnp.mean(o * o, axis=-1, keepdims=True)
    o = (o * lax.rsqrt(ms + RMS_EPS) * g_ref[...]) * (1.0 - lam_init)
    o_ref[...] = o.astype(o_ref.dtype)


def _attn_a(proj, bias, lq1, lk1, lq2, lk2, subln_g, batch, seq, lam_init):
    tk = ATTN_A_TK
    t = _q_tile(ATTN_A_T, seq, tk)
    qt = seq // t
    lam_spec = pl.BlockSpec((1, A_QK_DIM), lambda b, h, i: (0, 0))
    return pl.pallas_call(
        functools.partial(_attn_a_kernel, n_chunks=seq // tk, tk=tk, lam_init=lam_init),
        grid=(batch, A_HEADS, qt),
        in_specs=[
            pl.BlockSpec((t, HEAD_W), lambda b, h, i: (b * qt + i, h)),
            pl.BlockSpec((seq, HEAD_W), lambda b, h, i: (b, A_HEADS + h)),
            pl.BlockSpec((seq, HEAD_W), lambda b, h, i: (b, 2 * A_HEADS + h)),
            pl.BlockSpec((1, BIAS_TILES, ROW_BLOCK, tk), lambda b, h, i: (h, 0, 0, 0)),
            lam_spec, lam_spec, lam_spec, lam_spec,
            pl.BlockSpec((1, HEAD_W), lambda b, h, i: (0, 0)),
        ],
        out_specs=pl.BlockSpec((t, HEAD_W), lambda b, h, i: (b * qt + i, h)),
        out_shape=jax.ShapeDtypeStruct((batch * seq, A_HEADS * HEAD_W), BF16),
        scratch_shapes=_softmax_scratch(2 * t),
        compiler_params=_params(3),
        name="attn_a",
    )(proj, proj, proj, bias, lq1, lk1, lq2, lk2, subln_g)


def _attn_b_kernel(q_ref, k_ref, v_ref, o_ref, m_ref, acc_ref, *, n_chunks, tk):
    t = q_ref.shape[0]
    q = jnp.concatenate([q_ref[:, h * HEAD_W:(h + 1) * HEAD_W] for h in range(B_GROUP)], axis=0)

    _online_softmax_loop(q, k_ref, v_ref, n_chunks, tk, lambda s, j, r0: s,
                         m_ref, acc_ref)

    o = acc_ref[:, 0:HEAD_W] / acc_ref[:, HEAD_W:2 * HEAD_W]
    for h in range(B_GROUP):
        o_ref[:, h * HEAD_W:(h + 1) * HEAD_W] = o[h * t:(h + 1) * t, :].astype(o_ref.dtype)


def _attn_b(proj, batch, seq):
    tk = ATTN_B_TK
    t = _q_tile(ATTN_B_T, seq, tk)
    qt = seq // t
    gw = B_GROUP * HEAD_W
    return pl.pallas_call(
        functools.partial(_attn_b_kernel, n_chunks=seq // tk, tk=tk),
        grid=(batch, B_KV_HEADS, qt),
        in_specs=[
            pl.BlockSpec((t, gw), lambda b, g, i: (b * qt + i, g)),
            pl.BlockSpec((seq, HEAD_W), lambda b, g, i: (b, B_HEADS + g)),
            pl.BlockSpec((seq, HEAD_W), lambda b, g, i: (b, B_HEADS + B_KV_HEADS + g)),
        ],
        out_specs=pl.BlockSpec((t, gw), lambda b, g, i: (b * qt + i, g)),
        out_shape=jax.ShapeDtypeStruct((batch * seq, B_HEADS * HEAD_W), BF16),
        scratch_shapes=_softmax_scratch(B_GROUP * t),
        compiler_params=_params(3),
        name="attn_b",
    )(proj, proj, proj)


def _layer_norm(x, g, b):
    mu = jnp.mean(x, axis=-1, keepdims=True)
    xc = x - mu
    var = jnp.mean(xc * xc, axis=-1, keepdims=True)
    return xc * lax.rsqrt(var + LN_EPS) * g + b


def _post_attn_kernel(x_ref, oa_ref, ob_ref, wga_ref, wgb_ref, wa_ref, wb_ref, wo_ref,
                      g_ref, b_ref, h_ref):
    x = x_ref[...]
    xb = x.astype(BF16)
    ga = jnp.dot(xb, wga_ref[...], preferred_element_type=F32)
    gb = jnp.dot(xb, wgb_ref[...], preferred_element_type=F32)
    ta = jnp.dot(oa_ref[...], wa_ref[...], preferred_element_type=F32)
    tb = jnp.dot(ob_ref[...], wb_ref[...], preferred_element_type=F32)
    merged = jax.nn.sigmoid(ga) * ta + jax.nn.sigmoid(gb) * tb
    mo = jnp.dot(merged.astype(BF16), wo_ref[...], preferred_element_type=F32)
    h_ref[...] = _layer_norm(DEEPNORM_ALPHA * x + mo, g_ref[...], b_ref[...])


def _post_attn(x2d, oa, ob, wga, wgb, wa, wb, wo, g, b):
    n_tok = x2d.shape[0]
    tm = POST_TM
    row = lambda i: (i, 0)
    const = lambda i: (0, 0)
    wspec = _resident((D_MODEL, D_MODEL), const)
    vspec = pl.BlockSpec((1, D_MODEL), const)
    return pl.pallas_call(
        _post_attn_kernel,
        grid=(n_tok // tm,),
        in_specs=[
            pl.BlockSpec((tm, D_MODEL), row),
            pl.BlockSpec((tm, D_MODEL), row),
            pl.BlockSpec((tm, D_MODEL), row),
            wspec, wspec, wspec, wspec, wspec, vspec, vspec,
        ],
        out_specs=pl.BlockSpec((tm, D_MODEL), row),
        out_shape=jax.ShapeDtypeStruct((n_tok, D_MODEL), F32),
        compiler_params=_params(1),
        name="post_attn",
    )(x2d, oa, ob, wga, wgb, wa, wb, wo, g, b)


def _ffn_kernel(h_ref, wg_ref, wu_ref, wd_ref, g_ref, b_ref, y_ref):
    h = h_ref[...]
    hb = h.astype(BF16)
    gate = jnp.dot(hb, wg_ref[...], preferred_element_type=F32)
    up = jnp.dot(hb, wu_ref[...], preferred_element_type=F32)
    act = jax.nn.silu(gate) * up
    f = jnp.dot(act.astype(BF16), wd_ref[...], preferred_element_type=F32)
    y_ref[...] = _layer_norm(DEEPNORM_ALPHA * h + f, g_ref[...], b_ref[...])


def _ffn(h2d, wg, wu, wd, g, b):
    n_tok = h2d.shape[0]
    d_ff = wg.shape[1]
    tm = FFN_TM
    row = lambda i: (i, 0)
    const = lambda i: (0, 0)
    vspec = pl.BlockSpec((1, D_MODEL), const)
    return pl.pallas_call(
        _ffn_kernel,
        grid=(n_tok // tm,),
        in_specs=[
            pl.BlockSpec((tm, D_MODEL), row),
            _resident((D_MODEL, d_ff), const),
            _resident((D_MODEL, d_ff), const),
            _resident((d_ff, D_MODEL), const),
            vspec, vspec,
        ],
        out_specs=pl.BlockSpec((tm, D_MODEL), row),
        out_shape=jax.ShapeDtypeStruct((n_tok, D_MODEL), F32),
        compiler_params=_params(1),
        name="ffn",
    )(h2d, wg, wu, wd, g, b)


def _t5_bucket(rel):
    nb = REL_BUCKETS // 2
    max_exact = nb // 2
    ret = (rel > 0).astype(jnp.int32) * nb
    n = jnp.abs(rel)
    nf = jnp.maximum(n, 1).astype(F32)
    large = max_exact + (jnp.log(nf / max_exact) / math.log(REL_MAX_DIST / max_exact)
                         * (nb - max_exact)).astype(jnp.int32)
    large = jnp.minimum(large, nb - 1)
    return ret + jnp.where(n < max_exact, n, large)


def _bucket_tiles():
    r = jnp.arange(ROW_BLOCK, dtype=jnp.int32)[None, :, None]
    c = jnp.arange(ATTN_A_TK, dtype=jnp.int32)[None, None, :]
    off = (jnp.arange(BIAS_TILES, dtype=jnp.int32) + (BIAS_OFF_LO - 1))[:, None, None] * ROW_BLOCK
    return _t5_bucket(c - r + off)


def _rope_tables(seq):
    tok = jnp.arange(seq)
    row = (tok // GRID_W).astype(F32)
    col = (tok % GRID_W).astype(F32)
    freqs = ROPE_THETA ** (-jnp.arange(0, ROPE_AXIS_DIM, 2, dtype=F32) / ROPE_AXIS_DIM)
    ang_r = row[:, None] * freqs[None, :]
    ang_c = col[:, None] * freqs[None, :]
    cos = jnp.concatenate([jnp.cos(ang_r), jnp.cos(ang_r), jnp.cos(ang_c), jnp.cos(ang_c)], axis=-1)
    sin = jnp.concatenate([-jnp.sin(ang_r), jnp.sin(ang_r), -jnp.sin(ang_c), jnp.sin(ang_c)], axis=-1)
    return cos[:, ROPE_PERM], sin[:, ROPE_PERM]


def kernel(x_prompt, x_sample, w_in, lambda_q1, lambda_k1, lambda_q2, lambda_k2, subln_g,
           q_norm_g, k_norm_g, rel_bias, w_proj_a, w_proj_b, w_o, ln1_g, ln1_b,
           w_gate, w_up, w_down, ln2_g, ln2_b):
    assert w_in.shape[0] == DEPTH == 1
    lam_init = 0.8 - 0.6 * math.exp(-0.3 * 0)

    n_rope_cols = (B_HEADS + B_KV_HEADS) * HEAD_W
    w_rope = w_in[0][:, PROJ_A_COLS:PROJ_A_COLS + n_rope_cols].reshape(D_MODEL, -1, HEAD_W)
    w_main = jnp.concatenate([w_in[0][:, :PROJ_A_COLS],
                              w_rope[:, :, ROPE_PERM].reshape(D_MODEL, n_rope_cols),
                              w_in[0][:, PROJ_A_COLS + n_rope_cols:PROJ_COLS]], axis=1).astype(BF16)
    q_norm_g = q_norm_g[:, ROPE_PERM]
    k_norm_g = k_norm_g[:, ROPE_PERM]
    wga = w_in[0][:, PROJ_COLS:PROJ_COLS + D_MODEL].astype(BF16)
    wgb = w_in[0][:, PROJ_COLS + D_MODEL:].astype(BF16)
    wa = w_proj_a[0].astype(BF16)
    wb = w_proj_b[0].astype(BF16)
    wo = w_o[0].astype(BF16)
    wg = w_gate[0].astype(BF16)
    wu = w_up[0].astype(BF16)
    wd = w_down[0].astype(BF16)

    col_scale = jnp.concatenate([jnp.full((1, A_Q_COLS), A_Q_SCALE, F32),
                                 jnp.ones((1, PROJ_A_COLS - A_Q_COLS), F32)], axis=1)
    bias = _bias_tiles(rel_bias, _bucket_tiles())

    def run(x):
        batch, seq, _ = x.shape
        x2d = x.reshape(batch * seq, D_MODEL)
        cos, sin = _rope_tables(seq)
        proj_a, proj_b = _in_proj(x2d, w_main, col_scale, cos, sin, q_norm_g, k_norm_g, seq)
        oa = _attn_a(proj_a, bias, lambda_q1, lambda_k1, lambda_q2, lambda_k2, subln_g,
                     batch, seq, lam_init)
        ob = _attn_b(proj_b, batch, seq)
        h = _post_attn(x2d, oa, ob, wga, wgb, wa, wb, wo, ln1_g, ln1_b)
        y = _ffn(h, wg, wu, wd, ln2_g, ln2_b)
        return y.reshape(batch, seq, D_MODEL)

    return (run(x_prompt), run(x_sample))
```

```python
import functools
import math

import jax
import jax.numpy as jnp
import numpy as np
from jax import lax
from jax.experimental import pallas as pl
from jax.experimental.pallas import tpu as pltpu

F32 = jnp.float32
BF16 = jnp.bfloat16

D_MODEL = 1024
GRID_W = 64
HEAD_W = 128
A_HEADS = 8
A_QK_DIM = 64
B_HEADS = 8
B_KV_HEADS = 2
B_GROUP = B_HEADS // B_KV_HEADS
ROPE_THETA = 10000.0
ROPE_AXIS_DIM = HEAD_W // 2
REL_BUCKETS = 32
REL_MAX_DIST = 128
LN_EPS = 1e-5
RMS_EPS = 1e-6
DEPTH = 1
DEEPNORM_ALPHA = (2 * DEPTH) ** 0.25
LOG2E = math.log2(math.e)
A_Q_SCALE = A_QK_DIM ** -0.5 * LOG2E
B_Q_SCALE = HEAD_W ** -0.5 * LOG2E
ROPE_PERM = np.array([q * (ROPE_AXIS_DIM // 2) + i for q in (0, 2, 1, 3) for i in range(ROPE_AXIS_DIM // 2)],
                     dtype=np.int32)

A_Q_COLS = A_HEADS * HEAD_W
PROJ_A_COLS = 3 * A_Q_COLS
PROJ_B_COLS = (B_HEADS + 2 * B_KV_HEADS) * HEAD_W
PROJ_COLS = PROJ_A_COLS + PROJ_B_COLS
PROJ_TN = PROJ_B_COLS

VMEM_LIMIT_BYTES = 48 * 1024 * 1024

PROJ_TM = 512
ATTN_A_T = 1024
ATTN_A_TK = 512
ATTN_B_T = 512
ATTN_B_TK = 512
POST_TM = 512
FFN_TM = 512
ROW_BLOCK = 128
MAX_CHUNK_UNROLL = 5

BIAS_OFF_HI = (ROW_BLOCK + REL_MAX_DIST - 2) // ROW_BLOCK
BIAS_OFF_LO = -((ATTN_A_TK + REL_MAX_DIST - 2) // ROW_BLOCK)
BIAS_TILES = BIAS_OFF_HI - BIAS_OFF_LO + 3


def _params(n_axes):
    return pltpu.CompilerParams(dimension_semantics=("arbitrary",) * n_axes,
                                vmem_limit_bytes=VMEM_LIMIT_BYTES)


def _resident(shape, index_map):
    return pl.BlockSpec(shape, index_map, pipeline_mode=pl.Buffered(1))


def _in_proj_kernel(x_ref, w_ref, cs_ref, cos_ref, sin_ref, gq_ref, gk_ref, oa_ref, ob_ref):
    xb = x_ref[...].astype(BF16)

    def matmul(c0, c1):
        return jnp.dot(xb, w_ref[:, c0:c1], preferred_element_type=F32)

    res_b = matmul(PROJ_A_COLS, PROJ_COLS)
    for c0 in range(0, PROJ_A_COLS, PROJ_TN):
        c1 = c0 + PROJ_TN
        res_a = (matmul(c0, c1) * cs_ref[:, c0:c1]).astype(BF16)
        for h in range(PROJ_TN // HEAD_W):
            oa_ref[c0 // HEAD_W + h] = res_a[:, h * HEAD_W:(h + 1) * HEAD_W]

    cos = cos_ref[...]
    sin = sin_ref[...]
    n_rope = B_HEADS + B_KV_HEADS
    for h in range(PROJ_B_COLS // HEAD_W):
        v = res_b[:, h * HEAD_W:(h + 1) * HEAD_W]
        if h < n_rope:
            g = gq_ref[...] if h < B_HEADS else gk_ref[...]
            ms = jnp.mean(v * v, axis=-1, keepdims=True)
            vn = v * lax.rsqrt(ms + RMS_EPS) * g
            v = vn * cos + pltpu.roll(vn, HEAD_W // 2, 1) * sin
            if h < B_HEADS:
                v = v * B_Q_SCALE
        ob_ref[h] = v.astype(BF16)


def _in_proj(x2d, w, col_scale, cos, sin, gq, gk, seq):
    n_tok = x2d.shape[0]
    tm = PROJ_TM
    tiles_per_seq = seq // tm
    row = lambda i: (i, 0)
    const = lambda i: (0, 0)
    return pl.pallas_call(
        _in_proj_kernel,
        grid=(n_tok // tm,),
        in_specs=[
            pl.BlockSpec((tm, D_MODEL), row),
            _resident((D_MODEL, PROJ_COLS), const),
            pl.BlockSpec((1, PROJ_A_COLS), const),
            pl.BlockSpec((tm, HEAD_W), lambda i: (i % tiles_per_seq, 0)),
            pl.BlockSpec((tm, HEAD_W), lambda i: (i % tiles_per_seq, 0)),
            pl.BlockSpec((1, HEAD_W), const),
            pl.BlockSpec((1, HEAD_W), const),
        ],
        out_specs=[pl.BlockSpec((PROJ_A_COLS // HEAD_W, tm, HEAD_W), lambda i: (0, i, 0)),
                   pl.BlockSpec((PROJ_B_COLS // HEAD_W, tm, HEAD_W), lambda i: (0, i, 0))],
        out_shape=[jax.ShapeDtypeStruct((PROJ_A_COLS // HEAD_W, n_tok, HEAD_W), BF16),
                   jax.ShapeDtypeStruct((PROJ_B_COLS // HEAD_W, n_tok, HEAD_W), BF16)],
        compiler_params=_params(1),
        name="in_proj",
    )(x2d, w, col_scale, cos, sin, gq, gk)


def _bias_tiles_kernel(rb_ref, bucket_ref, o_ref):
    h = pl.program_id(0)
    bucket = bucket_ref[0]
    acc = jnp.zeros(bucket.shape, F32)
    for b in range(REL_BUCKETS):
        acc = jnp.where(bucket == b, rb_ref[b, h], acc)
    o_ref[0, 0] = acc * LOG2E


def _bias_tiles(rel_bias, bucket):
    n_tiles, rb, tk = bucket.shape
    return pl.pallas_call(
        _bias_tiles_kernel,
        grid=(A_HEADS, n_tiles),
        in_specs=[
            pl.BlockSpec(memory_space=pltpu.SMEM),
            pl.BlockSpec((1, rb, tk), lambda h, d: (d, 0, 0)),
        ],
        out_specs=pl.BlockSpec((1, 1, rb, tk), lambda h, d: (h, d, 0, 0)),
        out_shape=jax.ShapeDtypeStruct((A_HEADS, n_tiles, rb, tk), F32),
        compiler_params=_params(2),
        name="t5_bias_tiles",
    )(rel_bias, bucket)


def _online_softmax_loop(q, k_ref, v_ref, n_chunks, tk, score_fn, m_ref, acc_ref):
    n_rows = q.shape[0]
    lane_blocks = tk // HEAD_W
    ones = jnp.ones((tk, HEAD_W), BF16)

    def chunk(j, first):
        start = pl.multiple_of(j * tk, tk)
        ks = k_ref[pl.ds(start, tk), :]
        v_ext = jnp.concatenate([v_ref[pl.ds(start, tk), :], ones], axis=1)
        s_all = lax.dot_general(q, ks, (((1,), (1,)), ((), ())), preferred_element_type=F32)
        for r0 in range(0, n_rows, ROW_BLOCK):
            rows = slice(r0, r0 + ROW_BLOCK)
            s = score_fn(s_all[rows], j, r0)
            m_cur = jnp.max(s, axis=-1, keepdims=True)
            if first:
                m_new = jnp.broadcast_to(m_cur, (ROW_BLOCK, HEAD_W))
            else:
                m_prev = m_ref[rows, :]
                m_new = jnp.maximum(m_prev, m_cur)
            p = jnp.exp2((s - jnp.tile(m_new, (1, lane_blocks))).astype(BF16))
            pv = jnp.dot(p, v_ext, preferred_element_type=F32)
            if first:
                acc_ref[rows, :] = pv
            else:
                alpha = jnp.exp2(m_prev - m_new)
                acc_ref[rows, :] = jnp.tile(alpha, (1, 2)) * acc_ref[rows, :] + pv
            m_ref[rows, :] = m_new

    chunk(0, True)
    rest = n_chunks - 1
    unroll = max(u for u in range(1, MAX_CHUNK_UNROLL + 1) if rest % u == 0)

    def body(j, carry):
        chunk(j, False)
        return carry

    lax.fori_loop(1, n_chunks, body, 0, unroll=unroll)


def _softmax_scratch(n_rows):
    return [
        pltpu.VMEM((n_rows, HEAD_W), F32),
        pltpu.VMEM((n_rows, 2 * HEAD_W), F32),
    ]


def _attn_a_kernel(q_ref, k_ref, v_ref, bias_ref, lq1_ref, lk1_ref, lq2_ref, lk2_ref, g_ref,
                   o_ref, m_ref, acc_ref, *, n_chunks, tk, lam_init):
    t = q_ref.shape[0]
    i = pl.program_id(2)
    q = q_ref[...]
    lane = lax.broadcasted_iota(jnp.int32, q.shape, 1)
    zero = jnp.zeros_like(q)
    q2m = jnp.concatenate([jnp.where(lane < A_QK_DIM, q, zero),
                           jnp.where(lane >= A_QK_DIM, q, zero)], axis=0)

    def score_fn(s, j, r0):
        off = j * (tk // ROW_BLOCK) - (i * (t // ROW_BLOCK) + (r0 % t) // ROW_BLOCK)
        d = jnp.clip(off, BIAS_OFF_LO - 1, BIAS_OFF_HI + 1) - (BIAS_OFF_LO - 1)
        return s + bias_ref[0, d]

    _online_softmax_loop(q2m, k_ref, v_ref, n_chunks, tk, score_fn, m_ref, acc_ref)

    lam = (jnp.exp(jnp.sum(lq1_ref[...] * lk1_ref[...], axis=-1, keepdims=True))
           - jnp.exp(jnp.sum(lq2_ref[...] * lk2_ref[...], axis=-1, keepdims=True))
           + lam_init)
    o1 = acc_ref[0:t, 0:HEAD_W] / acc_ref[0:t, HEAD_W:2 * HEAD_W]
    o2 = acc_ref[t:2 * t, 0:HEAD_W] / acc_ref[t:2 * t, HEAD_W:2 * HEAD_W]
    o = o1 - lam * o2
    ms = jnp.mean(o * o, axis=-1, keepdims=True)
    o = (o * lax.rsqrt(ms + RMS_EPS) * g_ref[...]) * (1.0 - lam_init)
    o_ref[...] = o.astype(o_ref.dtype)


def _attn_a(proj, bias, lq1, lk1, lq2, lk2, subln_g, batch, seq, lam_init):
    tk = ATTN_A_TK
    t = min(seq, ATTN_A_T)
    qt = seq // t
    lam_spec = pl.BlockSpec((1, A_QK_DIM), lambda b, h, i: (0, 0))
    return pl.pallas_call(
        functools.partial(_attn_a_kernel, n_chunks=seq // tk, tk=tk, lam_init=lam_init),
        grid=(batch, A_HEADS, qt),
        in_specs=[
            pl.BlockSpec((None, t, HEAD_W), lambda b, h, i: (h, b * qt + i, 0)),
            pl.BlockSpec((None, seq, HEAD_W), lambda b, h, i: (A_HEADS + h, b, 0)),
            pl.BlockSpec((None, seq, HEAD_W), lambda b, h, i: (2 * A_HEADS + h, b, 0)),
            pl.BlockSpec((1, BIAS_TILES, ROW_BLOCK, tk), lambda b, h, i: (h, 0, 0, 0)),
            lam_spec, lam_spec, lam_spec, lam_spec,
            pl.BlockSpec((1, HEAD_W), lambda b, h, i: (0, 0)),
        ],
        out_specs=pl.BlockSpec((t, HEAD_W), lambda b, h, i: (b * qt + i, h)),
        out_shape=jax.ShapeDtypeStruct((batch * seq, A_HEADS * HEAD_W), BF16),
        scratch_shapes=_softmax_scratch(2 * t),
        compiler_params=_params(3),
        name="attn_a",
    )(proj, proj, proj, bias, lq1, lk1, lq2, lk2, subln_g)


def _attn_b_kernel(q_ref, k_ref, v_ref, o_ref, m_ref, acc_ref, *, n_chunks, tk):
    t = q_ref.shape[1]
    q = q_ref[...].reshape(B_GROUP * t, HEAD_W)

    _online_softmax_loop(q, k_ref, v_ref, n_chunks, tk, lambda s, j, r0: s,
                         m_ref, acc_ref)

    o = acc_ref[:, 0:HEAD_W] / acc_ref[:, HEAD_W:2 * HEAD_W]
    for h in range(B_GROUP):
        o_ref[:, h * HEAD_W:(h + 1) * HEAD_W] = o[h * t:(h + 1) * t, :].astype(o_ref.dtype)


def _attn_b(proj, batch, seq):
    tk = ATTN_B_TK
    t = min(seq, ATTN_B_T)
    qt = seq // t
    gw = B_GROUP * HEAD_W
    return pl.pallas_call(
        functools.partial(_attn_b_kernel, n_chunks=seq // tk, tk=tk),
        grid=(batch, B_KV_HEADS, qt),
        in_specs=[
            pl.BlockSpec((B_GROUP, t, HEAD_W), lambda b, g, i: (g, b * qt + i, 0)),
            pl.BlockSpec((None, seq, HEAD_W), lambda b, g, i: (B_HEADS + g, b, 0)),
            pl.BlockSpec((None, seq, HEAD_W), lambda b, g, i: (B_HEADS + B_KV_HEADS + g, b, 0)),
        ],
        out_specs=pl.BlockSpec((t, gw), lambda b, g, i: (b * qt + i, g)),
        out_shape=jax.ShapeDtypeStruct((batch * seq, B_HEADS * HEAD_W), BF16),
        scratch_shapes=_softmax_scratch(B_GROUP * t),
        compiler_params=_params(3),
        name="attn_b",
    )(proj, proj, proj)


def _layer_norm(x, g, b):
    mu = jnp.mean(x, axis=-1, keepdims=True)
    xc = x - mu
    var = jnp.mean(xc * xc, axis=-1, keepdims=True)
    return xc * lax.rsqrt(var + LN_EPS) * g + b


def _post_attn_kernel(x_ref, oa_ref, ob_ref, wga_ref, wgb_ref, wa_ref, wb_ref, wo_ref,
                      g_ref, b_ref, h_ref):
    x = x_ref[...]
    xb = x.astype(BF16)
    ga = jnp.dot(xb, wga_ref[...], preferred_element_type=F32)
    gb = jnp.dot(xb, wgb_ref[...], preferred_element_type=F32)
    ta = jnp.dot(oa_ref[...], wa_ref[...], preferred_element_type=F32)
    tb = jnp.dot(ob_ref[...], wb_ref[...], preferred_element_type=F32)
    merged = jax.nn.sigmoid(ga) * ta + jax.nn.sigmoid(gb) * tb
    mo = jnp.dot(merged.astype(BF16), wo_ref[...], preferred_element_type=F32)
    h_ref[...] = _layer_norm(DEEPNORM_ALPHA * x + mo, g_ref[...], b_ref[...])


def _post_attn(x2d, oa, ob, wga, wgb, wa, wb, wo, g, b):
    n_tok = x2d.shape[0]
    tm = POST_TM
    row = lambda i: (i, 0)
    const = lambda i: (0, 0)
    wspec = _resident((D_MODEL, D_MODEL), const)
    vspec = pl.BlockSpec((1, D_MODEL), const)
    return pl.pallas_call(
        _post_attn_kernel,
        grid=(n_tok // tm,),
        in_specs=[
            pl.BlockSpec((tm, D_MODEL), row),
            pl.BlockSpec((tm, D_MODEL), row),
            pl.BlockSpec((tm, D_MODEL), row),
            wspec, wspec, wspec, wspec, wspec, vspec, vspec,
        ],
        out_specs=pl.BlockSpec((tm, D_MODEL), row),
        out_shape=jax.ShapeDtypeStruct((n_tok, D_MODEL), F32),
        compiler_params=_params(1),
        name="post_attn",
    )(x2d, oa, ob, wga, wgb, wa, wb, wo, g, b)


def _ffn_kernel(h_ref, wg_ref, wu_ref, wd_ref, g_ref, b_ref, y_ref):
    h = h_ref[...]
    hb = h.astype(BF16)
    gate = jnp.dot(hb, wg_ref[...], preferred_element_type=F32)
    up = jnp.dot(hb, wu_ref[...], preferred_element_type=F32)
    act = jax.nn.silu(gate) * up
    f = jnp.dot(act.astype(BF16), wd_ref[...], preferred_element_type=F32)
    y_ref[...] = _layer_norm(DEEPNORM_ALPHA * h + f, g_ref[...], b_ref[...])


def _ffn(h2d, wg, wu, wd, g, b):
    n_tok = h2d.shape[0]
    d_ff = wg.shape[1]
    tm = FFN_TM
    row = lambda i: (i, 0)
    const = lambda i: (0, 0)
    vspec = pl.BlockSpec((1, D_MODEL), const)
    return pl.pallas_call(
        _ffn_kernel,
        grid=(n_tok // tm,),
        in_specs=[
            pl.BlockSpec((tm, D_MODEL), row),
            _resident((D_MODEL, d_ff), const),
            _resident((D_MODEL, d_ff), const),
            _resident((d_ff, D_MODEL), const),
            vspec, vspec,
        ],
        out_specs=pl.BlockSpec((tm, D_MODEL), row),
        out_shape=jax.ShapeDtypeStruct((n_tok, D_MODEL), F32),
        compiler_params=_params(1),
        name="ffn",
    )(h2d, wg, wu, wd, g, b)


def _t5_bucket(rel):
    nb = REL_BUCKETS // 2
    max_exact = nb // 2
    ret = (rel > 0).astype(jnp.int32) * nb
    n = jnp.abs(rel)
    nf = jnp.maximum(n, 1).astype(F32)
    large = max_exact + (jnp.log(nf / max_exact) / math.log(REL_MAX_DIST / max_exact)
                         * (nb - max_exact)).astype(jnp.int32)
    large = jnp.minimum(large, nb - 1)
    return ret + jnp.where(n < max_exact, n, large)


def _bucket_tiles():
    r = jnp.arange(ROW_BLOCK, dtype=jnp.int32)[None, :, None]
    c = jnp.arange(ATTN_A_TK, dtype=jnp.int32)[None, None, :]
    off = (jnp.arange(BIAS_TILES, dtype=jnp.int32) + (BIAS_OFF_LO - 1))[:, None, None] * ROW_BLOCK
    return _t5_bucket(c - r + off)


def _rope_tables(seq):
    tok = jnp.arange(seq)
    row = (tok // GRID_W).astype(F32)
    col = (tok % GRID_W).astype(F32)
    freqs = ROPE_THETA ** (-jnp.arange(0, ROPE_AXIS_DIM, 2, dtype=F32) / ROPE_AXIS_DIM)
    ang_r = row[:, None] * freqs[None, :]
    ang_c = col[:, None] * freqs[None, :]
    cos = jnp.concatenate([jnp.cos(ang_r), jnp.cos(ang_r), jnp.cos(ang_c), jnp.cos(ang_c)], axis=-1)
    sin = jnp.concatenate([-jnp.sin(ang_r), jnp.sin(ang_r), -jnp.sin(ang_c), jnp.sin(ang_c)], axis=-1)
    return cos[:, ROPE_PERM], sin[:, ROPE_PERM]


def kernel(x_prompt, x_sample, w_in, lambda_q1, lambda_k1, lambda_q2, lambda_k2, subln_g,
           q_norm_g, k_norm_g, rel_bias, w_proj_a, w_proj_b, w_o, ln1_g, ln1_b,
           w_gate, w_up, w_down, ln2_g, ln2_b):
    assert w_in.shape[0] == DEPTH == 1
    lam_init = 0.8 - 0.6 * math.exp(-0.3 * 0)

    n_rope_cols = (B_HEADS + B_KV_HEADS) * HEAD_W
    w_rope = w_in[0][:, PROJ_A_COLS:PROJ_A_COLS + n_rope_cols].reshape(D_MODEL, -1, HEAD_W)
    w_main = jnp.concatenate([w_in[0][:, :PROJ_A_COLS],
                              w_rope[:, :, ROPE_PERM].reshape(D_MODEL, n_rope_cols),
                              w_in[0][:, PROJ_A_COLS + n_rope_cols:PROJ_COLS]], axis=1).astype(BF16)
    q_norm_g = q_norm_g[:, ROPE_PERM]
    k_norm_g = k_norm_g[:, ROPE_PERM]
    wga = w_in[0][:, PROJ_COLS:PROJ_COLS + D_MODEL].astype(BF16)
    wgb = w_in[0][:, PROJ_COLS + D_MODEL:].astype(BF16)
    wa = w_proj_a[0].astype(BF16)
    wb = w_proj_b[0].astype(BF16)
    wo = w_o[0].astype(BF16)
    wg = w_gate[0].astype(BF16)
    wu = w_up[0].astype(BF16)
    wd = w_down[0].astype(BF16)

    col_scale = jnp.concatenate([jnp.full((1, A_Q_COLS), A_Q_SCALE, F32),
                                 jnp.ones((1, PROJ_A_COLS - A_Q_COLS), F32)], axis=1)
    bias = _bias_tiles(rel_bias, _bucket_tiles())

    def run(x):
        batch, seq, _ = x.shape
        x2d = x.reshape(batch * seq, D_MODEL)
        cos, sin = _rope_tables(seq)
        proj_a, proj_b = _in_proj(x2d, w_main, col_scale, cos, sin, q_norm_g, k_norm_g, seq)
        oa = _attn_a(proj_a, bias, lambda_q1, lambda_k1, lambda_q2, lambda_k2, subln_g,
                     batch, seq, lam_init)
        ob = _attn_b(proj_b, batch, seq)
        h = _post_attn(x2d, oa, ob, wga, wgb, wa, wb, wo, ln1_g, ln1_b)
        y = _ffn(h, wg, wu, wd, ln2_g, ln2_b)
        return y.reshape(batch, seq, D_MODEL)

    return (run(x_prompt), run(x_sample))
```

```python
import functools
import math

import jax
import jax.numpy as jnp
from jax import lax
from jax.experimental import pallas as pl
from jax.experimental.pallas import tpu as pltpu

F32 = jnp.float32
BF16 = jnp.bfloat16

D_MODEL = 1024
GRID_W = 64
HEAD_W = 128
A_HEADS = 8
A_QK_DIM = 64
B_HEADS = 8
B_KV_HEADS = 2
B_GROUP = B_HEADS // B_KV_HEADS
ROPE_THETA = 10000.0
ROPE_AXIS_DIM = HEAD_W // 2
REL_BUCKETS = 32
REL_MAX_DIST = 128
LN_EPS = 1e-5
RMS_EPS = 1e-6
DEPTH = 1
DEEPNORM_ALPHA = (2 * DEPTH) ** 0.25
LOG2E = math.log2(math.e)
A_Q_SCALE = A_QK_DIM ** -0.5 * LOG2E
B_Q_SCALE = HEAD_W ** -0.5 * LOG2E

A_Q_COLS = A_HEADS * HEAD_W
PROJ_A_COLS = 3 * A_Q_COLS
PROJ_B_COLS = (B_HEADS + 2 * B_KV_HEADS) * HEAD_W
PROJ_COLS = PROJ_A_COLS + PROJ_B_COLS
PROJ_TN = PROJ_B_COLS

VMEM_LIMIT_BYTES = 48 * 1024 * 1024

PROJ_TM = 512
ATTN_A_T = 2048
ATTN_A_TK = 512
ATTN_B_T = 1024
ATTN_B_TK = 512
POST_TM = 512
FFN_TM = 512
ROW_BLOCK = 128
MAX_CHUNK_UNROLL = 5

BIAS_OFF_HI = (ROW_BLOCK + REL_MAX_DIST - 2) // ROW_BLOCK
BIAS_OFF_LO = -((ATTN_A_TK + REL_MAX_DIST - 2) // ROW_BLOCK)
BIAS_TILES = BIAS_OFF_HI - BIAS_OFF_LO + 3


def _params(n_axes):
    return pltpu.CompilerParams(dimension_semantics=("arbitrary",) * n_axes,
                                vmem_limit_bytes=VMEM_LIMIT_BYTES)


def _resident(shape, index_map):
    return pl.BlockSpec(shape, index_map, pipeline_mode=pl.Buffered(1))


def _in_proj_kernel(x_ref, w_ref, cs_ref, cos_ref, sin_ref, gq_ref, gk_ref, oa_ref, ob_ref):
    xb = x_ref[...].astype(BF16)

    def matmul(c0, c1):
        return jnp.dot(xb, w_ref[:, c0:c1], preferred_element_type=F32)

    res_b = matmul(PROJ_A_COLS, PROJ_COLS)
    for c0 in range(0, PROJ_A_COLS, PROJ_TN):
        c1 = c0 + PROJ_TN
        res_a = (matmul(c0, c1) * cs_ref[:, c0:c1]).astype(BF16)
        for h in range(PROJ_TN // HEAD_W):
            oa_ref[c0 // HEAD_W + h] = res_a[:, h * HEAD_W:(h + 1) * HEAD_W]

    cos = cos_ref[...]
    sin = sin_ref[...]
    lane = lax.broadcasted_iota(jnp.int32, cos.shape, 1)
    first_half = (lane % ROPE_AXIS_DIM) < (ROPE_AXIS_DIM // 2)
    n_rope = B_HEADS + B_KV_HEADS
    for h in range(PROJ_B_COLS // HEAD_W):
        v = res_b[:, h * HEAD_W:(h + 1) * HEAD_W]
        if h < n_rope:
            g = gq_ref[...] if h < B_HEADS else gk_ref[...]
            ms = jnp.mean(v * v, axis=-1, keepdims=True)
            vn = v * lax.rsqrt(ms + RMS_EPS) * g
            rot = jnp.where(first_half,
                            pltpu.roll(vn, HEAD_W - ROPE_AXIS_DIM // 2, 1),
                            pltpu.roll(vn, ROPE_AXIS_DIM // 2, 1))
            v = vn * cos + rot * sin
            if h < B_HEADS:
                v = v * B_Q_SCALE
        ob_ref[h] = v.astype(BF16)


def _in_proj(x2d, w, col_scale, cos, sin, gq, gk, seq):
    n_tok = x2d.shape[0]
    tm = PROJ_TM
    tiles_per_seq = seq // tm
    row = lambda i: (i, 0)
    const = lambda i: (0, 0)
    return pl.pallas_call(
        _in_proj_kernel,
        grid=(n_tok // tm,),
        in_specs=[
            pl.BlockSpec((tm, D_MODEL), row),
            _resident((D_MODEL, PROJ_COLS), const),
            pl.BlockSpec((1, PROJ_A_COLS), const),
            pl.BlockSpec((tm, HEAD_W), lambda i: (i % tiles_per_seq, 0)),
            pl.BlockSpec((tm, HEAD_W), lambda i: (i % tiles_per_seq, 0)),
            pl.BlockSpec((1, HEAD_W), const),
            pl.BlockSpec((1, HEAD_W), const),
        ],
        out_specs=[pl.BlockSpec((PROJ_A_COLS // HEAD_W, tm, HEAD_W), lambda i: (0, i, 0)),
                   pl.BlockSpec((PROJ_B_COLS // HEAD_W, tm, HEAD_W), lambda i: (0, i, 0))],
        out_shape=[jax.ShapeDtypeStruct((PROJ_A_COLS // HEAD_W, n_tok, HEAD_W), BF16),
                   jax.ShapeDtypeStruct((PROJ_B_COLS // HEAD_W, n_tok, HEAD_W), BF16)],
        compiler_params=_params(1),
        name="in_proj",
    )(x2d, w, col_scale, cos, sin, gq, gk)


def _bias_tiles_kernel(rb_ref, bucket_ref, o_ref):
    h = pl.program_id(0)
    bucket = bucket_ref[0]
    acc = jnp.zeros(bucket.shape, F32)
    for b in range(REL_BUCKETS):
        acc = jnp.where(bucket == b, rb_ref[b, h], acc)
    o_ref[0, 0] = acc * LOG2E


def _bias_tiles(rel_bias, bucket):
    n_tiles, rb, tk = bucket.shape
    return pl.pallas_call(
        _bias_tiles_kernel,
        grid=(A_HEADS, n_tiles),
        in_specs=[
            pl.BlockSpec(memory_space=pltpu.SMEM),
            pl.BlockSpec((1, rb, tk), lambda h, d: (d, 0, 0)),
        ],
        out_specs=pl.BlockSpec((1, 1, rb, tk), lambda h, d: (h, d, 0, 0)),
        out_shape=jax.ShapeDtypeStruct((A_HEADS, n_tiles, rb, tk), F32),
        compiler_params=_params(2),
        name="t5_bias_tiles",
    )(rel_bias, bucket)


def _online_softmax_loop(q, k_ref, v_ref, n_chunks, tk, score_fn, m_ref, acc_ref):
    n_rows = q.shape[0]
    lane_blocks = tk // HEAD_W
    ones = jnp.ones((tk, HEAD_W), BF16)

    def chunk(j, first):
        start = pl.multiple_of(j * tk, tk)
        ks = k_ref[pl.ds(start, tk), :]
        v_ext = jnp.concatenate([v_ref[pl.ds(start, tk), :], ones], axis=1)
        s_all = lax.dot_general(q, ks, (((1,), (1,)), ((), ())), preferred_element_type=F32)
        for r0 in range(0, n_rows, ROW_BLOCK):
            rows = slice(r0, r0 + ROW_BLOCK)
            s = score_fn(s_all[rows], j, r0)
            m_cur = jnp.max(s, axis=-1, keepdims=True)
            if first:
                m_new = jnp.broadcast_to(m_cur, (ROW_BLOCK, HEAD_W))
            else:
                m_prev = m_ref[rows, :]
                m_new = jnp.maximum(m_prev, m_cur)
            p = jnp.exp2((s - jnp.tile(m_new, (1, lane_blocks))).astype(BF16))
            pv = jnp.dot(p, v_ext, preferred_element_type=F32)
            if first:
                acc_ref[rows, :] = pv
            else:
                alpha = jnp.exp2(m_prev - m_new)
                acc_ref[rows, :] = jnp.tile(alpha, (1, 2)) * acc_ref[rows, :] + pv
            m_ref[rows, :] = m_new

    chunk(0, True)
    rest = n_chunks - 1
    unroll = max(u for u in range(1, MAX_CHUNK_UNROLL + 1) if rest % u == 0)

    def body(j, carry):
        chunk(j, False)
        return carry

    lax.fori_loop(1, n_chunks, body, 0, unroll=unroll)


def _q_tile(full_tile, seq, tk):
    return min(seq, full_tile if seq // tk - 1 <= MAX_CHUNK_UNROLL else full_tile // 2)


def _softmax_scratch(n_rows):
    return [
        pltpu.VMEM((n_rows, HEAD_W), F32),
        pltpu.VMEM((n_rows, 2 * HEAD_W), F32),
    ]


def _attn_a_kernel(q_ref, k_ref, v_ref, bias_ref, lq1_ref, lk1_ref, lq2_ref, lk2_ref, g_ref,
                   o_ref, m_ref, acc_ref, *, n_chunks, tk, lam_init):
    t = q_ref.shape[0]
    i = pl.program_id(2)
    q = q_ref[...]
    lane = lax.broadcasted_iota(jnp.int32, q.shape, 1)
    zero = jnp.zeros_like(q)
    q2m = jnp.concatenate([jnp.where(lane < A_QK_DIM, q, zero),
                           jnp.where(lane >= A_QK_DIM, q, zero)], axis=0)

    def score_fn(s, j, r0):
        off = j * (tk // ROW_BLOCK) - (i * (t // ROW_BLOCK) + (r0 % t) // ROW_BLOCK)
        d = jnp.clip(off, BIAS_OFF_LO - 1, BIAS_OFF_HI + 1) - (BIAS_OFF_LO - 1)
        return s + bias_ref[0, d]

    _online_softmax_loop(q2m, k_ref, v_ref, n_chunks, tk, score_fn, m_ref, acc_ref)

    lam = (jnp.exp(jnp.sum(lq1_ref[...] * lk1_ref[...], axis=-1, keepdims=True))
           - jnp.exp(jnp.sum(lq2_ref[...] * lk2_ref[...], axis=-1, keepdims=True))
           + lam_init)
    o1 = acc_ref[0:t, 0:HEAD_W] / acc_ref[0:t, HEAD_W:2 * HEAD_W]
    o2 = acc_ref[t:2 * t, 0:HEAD_W] / acc_ref[t:2 * t, HEAD_W:2 * HEAD_W]
    o = o1 - lam * o2
    ms = jnp.mean(o * o, axis=-1, keepdims=True)
    o = (o * lax.rsqrt(ms + RMS_EPS) * g_ref[...]) * (1.0 - lam_init)
    o_ref[...] = o.astype(o_ref.dtype)


def _attn_a(proj, bias, lq1, lk1, lq2, lk2, subln_g, batch, seq, lam_init):
    tk = ATTN_A_TK
    t = _q_tile(ATTN_A_T, seq, tk)
    qt = seq // t
    lam_spec = pl.BlockSpec((1, A_QK_DIM), lambda b, h, i: (0, 0))
    return pl.pallas_call(
        functools.partial(_attn_a_kernel, n_chunks=seq // tk, tk=tk, lam_init=lam_init),
        grid=(batch, A_HEADS, qt),
        in_specs=[
            pl.BlockSpec((None, t, HEAD_W), lambda b, h, i: (h, b * qt + i, 0)),
            pl.BlockSpec((None, seq, HEAD_W), lambda b, h, i: (A_HEADS + h, b, 0)),
            pl.BlockSpec((None, seq, HEAD_W), lambda b, h, i: (2 * A_HEADS + h, b, 0)),
            pl.BlockSpec((1, BIAS_TILES, ROW_BLOCK, tk), lambda b, h, i: (h, 0, 0, 0)),
            lam_spec, lam_spec, lam_spec, lam_spec,
            pl.BlockSpec((1, HEAD_W), lambda b, h, i: (0, 0)),
        ],
        out_specs=pl.BlockSpec((t, HEAD_W), lambda b, h, i: (b * qt + i, h)),
        out_shape=jax.ShapeDtypeStruct((batch * seq, A_HEADS * HEAD_W), BF16),
        scratch_shapes=_softmax_scratch(2 * t),
        compiler_params=_params(3),
        name="attn_a",
    )(proj, proj, proj, bias, lq1, lk1, lq2, lk2, subln_g)


def _attn_b_kernel(q_ref, k_ref, v_ref, o_ref, m_ref, acc_ref, *, n_chunks, tk):
    t = q_ref.shape[1]
    q = q_ref[...].reshape(B_GROUP * t, HEAD_W)

    _online_softmax_loop(q, k_ref, v_ref, n_chunks, tk, lambda s, j, r0: s,
                         m_ref, acc_ref)

    o = acc_ref[:, 0:HEAD_W] / acc_ref[:, HEAD_W:2 * HEAD_W]
    for h in range(B_GROUP):
        o_ref[:, h * HEAD_W:(h + 1) * HEAD_W] = o[h * t:(h + 1) * t, :].astype(o_ref.dtype)


def _attn_b(proj, batch, seq):
    tk = ATTN_B_TK
    t = _q_tile(ATTN_B_T, seq, tk)
    qt = seq // t
    gw = B_GROUP * HEAD_W
    return pl.pallas_call(
        functools.partial(_attn_b_kernel, n_chunks=seq // tk, tk=tk),
        grid=(batch, B_KV_HEADS, qt),
        in_specs=[
            pl.BlockSpec((B_GROUP, t, HEAD_W), lambda b, g, i: (g, b * qt + i, 0)),
            pl.BlockSpec((None, seq, HEAD_W), lambda b, g, i: (B_HEADS + g, b, 0)),
            pl.BlockSpec((None, seq, HEAD_W), lambda b, g, i: (B_HEADS + B_KV_HEADS + g, b, 0)),
        ],
        out_specs=pl.BlockSpec((t, gw), lambda b, g, i: (b * qt + i, g)),
        out_shape=jax.ShapeDtypeStruct((batch * seq, B_HEADS * HEAD_W), BF16),
        scratch_shapes=_softmax_scratch(B_GROUP * t),
        compiler_params=_params(3),
        name="attn_b",
    )(proj, proj, proj)


def _layer_norm(x, g, b):
    mu = jnp.mean(x, axis=-1, keepdims=True)
    xc = x - mu
    var = jnp.mean(xc * xc, axis=-1, keepdims=True)
    return xc * lax.rsqrt(var + LN_EPS) * g + b


def _post_attn_kernel(x_ref, oa_ref, ob_ref, wga_ref, wgb_ref, wa_ref, wb_ref, wo_ref,
                      g_ref, b_ref, h_ref):
    x = x_ref[...]
    xb = x.astype(BF16)
    ga = jnp.dot(xb, wga_ref[...], preferred_element_type=F32)
    gb = jnp.dot(xb, wgb_ref[...], preferred_element_type=F32)
    ta = jnp.dot(oa_ref[...], wa_ref[...], preferred_element_type=F32)
    tb = jnp.dot(ob_ref[...], wb_ref[...], preferred_element_type=F32)
    merged = jax.nn.sigmoid(ga) * ta + jax.nn.sigmoid(gb) * tb
    mo = jnp.dot(merged.astype(BF16), wo_ref[...], preferred_element_type=F32)
    h_ref[...] = _layer_norm(DEEPNORM_ALPHA * x + mo, g_ref[...], b_ref[...])


def _post_attn(x2d, oa, ob, wga, wgb, wa, wb, wo, g, b):
    n_tok = x2d.shape[0]
    tm = POST_TM
    row = lambda i: (i, 0)
    const = lambda i: (0, 0)
    wspec = _resident((D_MODEL, D_MODEL), const)
    vspec = pl.BlockSpec((1, D_MODEL), const)
    return pl.pallas_call(
        _post_attn_kernel,
        grid=(n_tok // tm,),
        in_specs=[
            pl.BlockSpec((tm, D_MODEL), row),
            pl.BlockSpec((tm, D_MODEL), row),
            pl.BlockSpec((tm, D_MODEL), row),
            wspec, wspec, wspec, wspec, wspec, vspec, vspec,
        ],
        out_specs=pl.BlockSpec((tm, D_MODEL), row),
        out_shape=jax.ShapeDtypeStruct((n_tok, D_MODEL), F32),
        compiler_params=_params(1),
        name="post_attn",
    )(x2d, oa, ob, wga, wgb, wa, wb, wo, g, b)


def _ffn_kernel(h_ref, wg_ref, wu_ref, wd_ref, g_ref, b_ref, y_ref):
    h = h_ref[...]
    hb = h.astype(BF16)
    gate = jnp.dot(hb, wg_ref[...], preferred_element_type=F32)
    up = jnp.dot(hb, wu_ref[...], preferred_element_type=F32)
    act = jax.nn.silu(gate) * up
    f = jnp.dot(act.astype(BF16), wd_ref[...], preferred_element_type=F32)
    y_ref[...] = _layer_norm(DEEPNORM_ALPHA * h + f, g_ref[...], b_ref[...])


def _ffn(h2d, wg, wu, wd, g, b):
    n_tok = h2d.shape[0]
    d_ff = wg.shape[1]
    tm = FFN_TM
    row = lambda i: (i, 0)
    const = lambda i: (0, 0)
    vspec = pl.BlockSpec((1, D_MODEL), const)
    return pl.pallas_call(
        _ffn_kernel,
        grid=(n_tok // tm,),
        in_specs=[
            pl.BlockSpec((tm, D_MODEL), row),
            _resident((D_MODEL, d_ff), const),
            _resident((D_MODEL, d_ff), const),
            _resident((d_ff, D_MODEL), const),
            vspec, vspec,
        ],
        out_specs=pl.BlockSpec((tm, D_MODEL), row),
        out_shape=jax.ShapeDtypeStruct((n_tok, D_MODEL), F32),
        compiler_params=_params(1),
        name="ffn",
    )(h2d, wg, wu, wd, g, b)


def _t5_bucket(rel):
    nb = REL_BUCKETS // 2
    max_exact = nb // 2
    ret = (rel > 0).astype(jnp.int32) * nb
    n = jnp.abs(rel)
    nf = jnp.maximum(n, 1).astype(F32)
    large = max_exact + (jnp.log(nf / max_exact) / math.log(REL_MAX_DIST / max_exact)
                         * (nb - max_exact)).astype(jnp.int32)
    large = jnp.minimum(large, nb - 1)
    return ret + jnp.where(n < max_exact, n, large)


def _bucket_tiles():
    r = jnp.arange(ROW_BLOCK, dtype=jnp.int32)[None, :, None]
    c = jnp.arange(ATTN_A_TK, dtype=jnp.int32)[None, None, :]
    off = (jnp.arange(BIAS_TILES, dtype=jnp.int32) + (BIAS_OFF_LO - 1))[:, None, None] * ROW_BLOCK
    return _t5_bucket(c - r + off)


def _rope_tables(seq):
    tok = jnp.arange(seq)
    row = (tok // GRID_W).astype(F32)
    col = (tok % GRID_W).astype(F32)
    freqs = ROPE_THETA ** (-jnp.arange(0, ROPE_AXIS_DIM, 2, dtype=F32) / ROPE_AXIS_DIM)
    ang_r = row[:, None] * freqs[None, :]
    ang_c = col[:, None] * freqs[None, :]
    cos = jnp.concatenate([jnp.cos(ang_r), jnp.cos(ang_r), jnp.cos(ang_c), jnp.cos(ang_c)], axis=-1)
    sin = jnp.concatenate([-jnp.sin(ang_r), jnp.sin(ang_r), -jnp.sin(ang_c), jnp.sin(ang_c)], axis=-1)
    return cos, sin


def kernel(x_prompt, x_sample, w_in, lambda_q1, lambda_k1, lambda_q2, lambda_k2, subln_g,
           q_norm_g, k_norm_g, rel_bias, w_proj_a, w_proj_b, w_o, ln1_g, ln1_b,
           w_gate, w_up, w_down, ln2_g, ln2_b):
    assert w_in.shape[0] == DEPTH == 1
    lam_init = 0.8 - 0.6 * math.exp(-0.3 * 0)

    w_main = w_in[0][:, :PROJ_COLS].astype(BF16)
    wga = w_in[0][:, PROJ_COLS:PROJ_COLS + D_MODEL].astype(BF16)
    wgb = w_in[0][:, PROJ_COLS + D_MODEL:].astype(BF16)
    wa = w_proj_a[0].astype(BF16)
    wb = w_proj_b[0].astype(BF16)
    wo = w_o[0].astype(BF16)
    wg = w_gate[0].astype(BF16)
    wu = w_up[0].astype(BF16)
    wd = w_down[0].astype(BF16)

    col_scale = jnp.concatenate([jnp.full((1, A_Q_COLS), A_Q_SCALE, F32),
                                 jnp.ones((1, PROJ_A_COLS - A_Q_COLS), F32)], axis=1)
    bias = _bias_tiles(rel_bias, _bucket_tiles())

    def run(x):
        batch, seq, _ = x.shape
        x2d = x.reshape(batch * seq, D_MODEL)
        cos, sin = _rope_tables(seq)
        proj_a, proj_b = _in_proj(x2d, w_main, col_scale, cos, sin, q_norm_g, k_norm_g, seq)
        oa = _attn_a(proj_a, bias, lambda_q1, lambda_k1, lambda_q2, lambda_k2, subln_g,
                     batch, seq, lam_init)
        ob = _attn_b(proj_b, batch, seq)
        h = _post_attn(x2d, oa, ob, wga, wgb, wa, wb, wo, ln1_g, ln1_b)
        y = _ffn(h, wg, wu, wd, ln2_g, ln2_b)
        return y.reshape(batch, seq, D_MODEL)

    return (run(x_prompt), run(x_sample))
```

```python
import functools
import math

import jax
import jax.numpy as jnp
from jax import lax
from jax.experimental import pallas as pl
from jax.experimental.pallas import tpu as pltpu

F32 = jnp.float32
BF16 = jnp.bfloat16

D_MODEL = 1024
GRID_W = 64
HEAD_W = 128
A_HEADS = 8
A_QK_DIM = 64
B_HEADS = 8
B_KV_HEADS = 2
B_GROUP = B_HEADS // B_KV_HEADS
ROPE_THETA = 10000.0
ROPE_AXIS_DIM = HEAD_W // 2
REL_BUCKETS = 32
REL_MAX_DIST = 128
LN_EPS = 1e-5
RMS_EPS = 1e-6
DEPTH = 1
DEEPNORM_ALPHA = (2 * DEPTH) ** 0.25
LOG2E = math.log2(math.e)
A_Q_SCALE = A_QK_DIM ** -0.5 * LOG2E
B_Q_SCALE = HEAD_W ** -0.5 * LOG2E

A_Q_COLS = A_HEADS * HEAD_W
PROJ_A_COLS = 3 * A_Q_COLS
PROJ_B_COLS = (B_HEADS + 2 * B_KV_HEADS) * HEAD_W
PROJ_COLS = PROJ_A_COLS + PROJ_B_COLS
PROJ_TN = PROJ_B_COLS

VMEM_LIMIT_BYTES = 48 * 1024 * 1024

PROJ_TM = 512
ATTN_A_T = 2048
ATTN_A_TK = 512
ATTN_B_T = 1024
ATTN_B_TK = 512
POST_TM = 512
FFN_TM = 512
ROW_BLOCK = 128
GAIN_ROWS = 16
MAX_CHUNK_UNROLL = 5

BIAS_OFF_HI = (ROW_BLOCK + REL_MAX_DIST - 2) // ROW_BLOCK
BIAS_OFF_LO = -((ATTN_A_TK + REL_MAX_DIST - 2) // ROW_BLOCK)
BIAS_TILES = BIAS_OFF_HI - BIAS_OFF_LO + 3


def _params(n_axes):
    return pltpu.CompilerParams(dimension_semantics=("arbitrary",) * n_axes,
                                vmem_limit_bytes=VMEM_LIMIT_BYTES)


def _resident(shape, index_map):
    return pl.BlockSpec(shape, index_map, pipeline_mode=pl.Buffered(1))


def _in_proj_kernel(x_ref, w_ref, cs_ref, cos_ref, sin_ref, gq_ref, gk_ref, oa_ref, ob_ref):
    xb = x_ref[...].astype(BF16)

    def matmul(c0, c1):
        return jnp.dot(xb, w_ref[:, c0:c1], preferred_element_type=F32)

    res_b = matmul(PROJ_A_COLS, PROJ_COLS)
    for c0 in range(0, PROJ_A_COLS, PROJ_TN):
        c1 = c0 + PROJ_TN
        res_a = (matmul(c0, c1) * cs_ref[:, c0:c1]).astype(BF16)
        for h in range(PROJ_TN // HEAD_W):
            oa_ref[c0 // HEAD_W + h] = res_a[:, h * HEAD_W:(h + 1) * HEAD_W]

    cos = cos_ref[...]
    sin = sin_ref[...]
    lane = lax.broadcasted_iota(jnp.int32, cos.shape, 1)
    first_half = (lane % ROPE_AXIS_DIM) < (ROPE_AXIS_DIM // 2)
    n_rope = B_HEADS + B_KV_HEADS
    for h in range(PROJ_B_COLS // HEAD_W):
        v = res_b[:, h * HEAD_W:(h + 1) * HEAD_W]
        if h < n_rope:
            g = gq_ref[...] if h < B_HEADS else gk_ref[...]
            ms = jnp.mean(v * v, axis=-1, keepdims=True)
            vn = v * lax.rsqrt(ms + RMS_EPS) * g
            rot = jnp.where(first_half,
                            pltpu.roll(vn, HEAD_W - ROPE_AXIS_DIM // 2, 1),
                            pltpu.roll(vn, ROPE_AXIS_DIM // 2, 1))
            v = vn * cos + rot * sin
            if h < B_HEADS:
                v = v * B_Q_SCALE
        ob_ref[h] = v.astype(BF16)


def _in_proj(x2d, w, col_scale, cos, sin, gq, gk, seq):
    n_tok = x2d.shape[0]
    tm = PROJ_TM
    tiles_per_seq = seq // tm
    row = lambda i: (i, 0)
    const = lambda i: (0, 0)
    return pl.pallas_call(
        _in_proj_kernel,
        grid=(n_tok // tm,),
        in_specs=[
            pl.BlockSpec((tm, D_MODEL), row),
            _resident((D_MODEL, PROJ_COLS), const),
            pl.BlockSpec((1, PROJ_A_COLS), const),
            pl.BlockSpec((tm, HEAD_W), lambda i: (i % tiles_per_seq, 0)),
            pl.BlockSpec((tm, HEAD_W), lambda i: (i % tiles_per_seq, 0)),
            pl.BlockSpec((1, HEAD_W), const),
            pl.BlockSpec((1, HEAD_W), const),
        ],
        out_specs=[pl.BlockSpec((PROJ_A_COLS // HEAD_W, tm, HEAD_W), lambda i: (0, i, 0)),
                   pl.BlockSpec((PROJ_B_COLS // HEAD_W, tm, HEAD_W), lambda i: (0, i, 0))],
        out_shape=[jax.ShapeDtypeStruct((PROJ_A_COLS // HEAD_W, n_tok, HEAD_W), BF16),
                   jax.ShapeDtypeStruct((PROJ_B_COLS // HEAD_W, n_tok, HEAD_W), BF16)],
        compiler_params=_params(1),
        name="in_proj",
    )(x2d, w, col_scale, cos, sin, gq, gk)


def _bias_tiles_kernel(rb_ref, bucket_ref, o_ref):
    h = pl.program_id(0)
    bucket = bucket_ref[0]
    acc = jnp.zeros(bucket.shape, F32)
    for b in range(REL_BUCKETS):
        acc = jnp.where(bucket == b, rb_ref[b, h], acc)
    o_ref[0, 0] = acc * LOG2E


def _bias_tiles(rel_bias, bucket):
    n_tiles, rb, tk = bucket.shape
    return pl.pallas_call(
        _bias_tiles_kernel,
        grid=(A_HEADS, n_tiles),
        in_specs=[
            pl.BlockSpec(memory_space=pltpu.SMEM),
            pl.BlockSpec((1, rb, tk), lambda h, d: (d, 0, 0)),
        ],
        out_specs=pl.BlockSpec((1, 1, rb, tk), lambda h, d: (h, d, 0, 0)),
        out_shape=jax.ShapeDtypeStruct((A_HEADS, n_tiles, rb, tk), F32),
        compiler_params=_params(2),
        name="t5_bias_tiles",
    )(rel_bias, bucket)


def _online_softmax_loop(q, k_ref, v_ref, n_chunks, tk, score_fn, m_ref, acc_ref):
    n_rows = q.shape[0]
    lane_blocks = tk // HEAD_W
    ones = jnp.ones((tk, HEAD_W), BF16)

    def chunk(j, first):
        start = pl.multiple_of(j * tk, tk)
        ks = k_ref[pl.ds(start, tk), :]
        v_ext = jnp.concatenate([v_ref[pl.ds(start, tk), :], ones], axis=1)
        s_all = lax.dot_general(q, ks, (((1,), (1,)), ((), ())), preferred_element_type=F32)
        for r0 in range(0, n_rows, ROW_BLOCK):
            rows = slice(r0, r0 + ROW_BLOCK)
            s = score_fn(s_all[rows], j, r0)
            m_cur = jnp.max(s, axis=-1, keepdims=True)
            if first:
                m_new = jnp.broadcast_to(m_cur, (ROW_BLOCK, HEAD_W))
            else:
                m_prev = m_ref[rows, :]
                m_new = jnp.maximum(m_prev, m_cur)
            p = jnp.exp2((s - jnp.tile(m_new, (1, lane_blocks))).astype(BF16))
            pv = jnp.dot(p, v_ext, preferred_element_type=F32)
            if first:
                acc_ref[rows, :] = pv
            else:
                alpha = jnp.exp2(m_prev - m_new)
                acc_ref[rows, :] = jnp.tile(alpha, (1, 2)) * acc_ref[rows, :] + pv
            m_ref[rows, :] = m_new

    chunk(0, True)
    rest = n_chunks - 1
    unroll = max(u for u in range(1, MAX_CHUNK_UNROLL + 1) if rest % u == 0)

    def body(j, carry):
        chunk(j, False)
        return carry

    lax.fori_loop(1, n_chunks, body, 0, unroll=unroll)


def _q_tile(full_tile, seq, tk):
    return min(seq, full_tile if seq // tk - 1 <= MAX_CHUNK_UNROLL else full_tile // 2)


def _softmax_scratch(n_rows):
    return [
        pltpu.VMEM((n_rows, HEAD_W), F32),
        pltpu.VMEM((n_rows, 2 * HEAD_W), F32),
    ]


def _attn_a_kernel(q_ref, k_ref, v_ref, bias_ref, lq1_ref, lk1_ref, lq2_ref, lk2_ref, g_ref,
                   o_ref, m_ref, acc_ref, *, n_chunks, tk, lam_init):
    t = q_ref.shape[0]
    i = pl.program_id(2)
    q = q_ref[...]
    lane = lax.broadcasted_iota(jnp.int32, q.shape, 1)
    zero = jnp.zeros_like(q)
    q2m = jnp.concatenate([jnp.where(lane < A_QK_DIM, q, zero),
                           jnp.where(lane >= A_QK_DIM, q, zero)], axis=0)

    def score_fn(s, j, r0):
        off = j * (tk // ROW_BLOCK) - (i * (t // ROW_BLOCK) + (r0 % t) // ROW_BLOCK)
        d = jnp.clip(off, BIAS_OFF_LO - 1, BIAS_OFF_HI + 1) - (BIAS_OFF_LO - 1)
        return s + bias_ref[0, d]

    _online_softmax_loop(q2m, k_ref, v_ref, n_chunks, tk, score_fn, m_ref, acc_ref)

    lam = (jnp.exp(jnp.sum(lq1_ref[...] * lk1_ref[...], axis=-1, keepdims=True))
           - jnp.exp(jnp.sum(lq2_ref[...] * lk2_ref[...], axis=-1, keepdims=True))
           + lam_init)
    o1 = acc_ref[0:t, 0:HEAD_W] / acc_ref[0:t, HEAD_W:2 * HEAD_W]
    o2 = acc_ref[t:2 * t, 0:HEAD_W] / acc_ref[t:2 * t, HEAD_W:2 * HEAD_W]
    o = o1 - lam * o2
    ms = jnp.mean(o * o, axis=-1, keepdims=True)
    o = (o * lax.rsqrt(ms + RMS_EPS) * g_ref[0:1, :]) * (1.0 - lam_init)
    o_ref[...] = o.astype(o_ref.dtype)


def _attn_a(proj, bias, lq1, lk1, lq2, lk2, subln_g, batch, seq, lam_init):
    tk = ATTN_A_TK
    t = _q_tile(ATTN_A_T, seq, tk)
    qt = seq // t
    lam_spec = pl.BlockSpec((1, A_QK_DIM), lambda b, h, i: (0, 0))
    return pl.pallas_call(
        functools.partial(_attn_a_kernel, n_chunks=seq // tk, tk=tk, lam_init=lam_init),
        grid=(batch, A_HEADS, qt),
        in_specs=[
            pl.BlockSpec((None, t, HEAD_W), lambda b, h, i: (h, b * qt + i, 0)),
            pl.BlockSpec((None, seq, HEAD_W), lambda b, h, i: (A_HEADS + h, b, 0)),
            pl.BlockSpec((None, seq, HEAD_W), lambda b, h, i: (2 * A_HEADS + h, b, 0)),
            pl.BlockSpec((1, BIAS_TILES, ROW_BLOCK, tk), lambda b, h, i: (h, 0, 0, 0)),
            lam_spec, lam_spec, lam_spec, lam_spec,
            pl.BlockSpec((GAIN_ROWS, HEAD_W), lambda b, h, i: (0, 0)),
        ],
        out_specs=pl.BlockSpec((t, HEAD_W), lambda b, h, i: (b * qt + i, h)),
        out_shape=jax.ShapeDtypeStruct((batch * seq, A_HEADS * HEAD_W), BF16),
        scratch_shapes=_softmax_scratch(2 * t),
        compiler_params=_params(3),
        name="attn_a",
    )(proj, proj, proj, bias, lq1, lk1, lq2, lk2, jnp.broadcast_to(subln_g, (GAIN_ROWS, HEAD_W)))


def _attn_b_kernel(q_ref, k_ref, v_ref, o_ref, m_ref, acc_ref, *, n_chunks, tk):
    t = q_ref.shape[1]
    q = q_ref[...].reshape(B_GROUP * t, HEAD_W)

    _online_softmax_loop(q, k_ref, v_ref, n_chunks, tk, lambda s, j, r0: s,
                         m_ref, acc_ref)

    o = acc_ref[:, 0:HEAD_W] / acc_ref[:, HEAD_W:2 * HEAD_W]
    for h in range(B_GROUP):
        o_ref[:, h * HEAD_W:(h + 1) * HEAD_W] = o[h * t:(h + 1) * t, :].astype(o_ref.dtype)


def _attn_b(proj, batch, seq):
    tk = ATTN_B_TK
    t = _q_tile(ATTN_B_T, seq, tk)
    qt = seq // t
    gw = B_GROUP * HEAD_W
    return pl.pallas_call(
        functools.partial(_attn_b_kernel, n_chunks=seq // tk, tk=tk),
        grid=(batch, B_KV_HEADS, qt),
        in_specs=[
            pl.BlockSpec((B_GROUP, t, HEAD_W), lambda b, g, i: (g, b * qt + i, 0)),
            pl.BlockSpec((None, seq, HEAD_W), lambda b, g, i: (B_HEADS + g, b, 0)),
            pl.BlockSpec((None, seq, HEAD_W), lambda b, g, i: (B_HEADS + B_KV_HEADS + g, b, 0)),
        ],
        out_specs=pl.BlockSpec((t, gw), lambda b, g, i: (b * qt + i, g)),
        out_shape=jax.ShapeDtypeStruct((batch * seq, B_HEADS * HEAD_W), BF16),
        scratch_shapes=_softmax_scratch(B_GROUP * t),
        compiler_params=_params(3),
        name="attn_b",
    )(proj, proj, proj)


def _layer_norm(x, g, b):
    mu = jnp.mean(x, axis=-1, keepdims=True)
    xc = x - mu
    var = jnp.mean(xc * xc, axis=-1, keepdims=True)
    return xc * lax.rsqrt(var + LN_EPS) * g + b


def _post_attn_kernel(x_ref, oa_ref, ob_ref, wga_ref, wgb_ref, wa_ref, wb_ref, wo_ref,
                      g_ref, b_ref, h_ref):
    x = x_ref[...]
    xb = x.astype(BF16)
    ga = jnp.dot(xb, wga_ref[...], preferred_element_type=F32)
    gb = jnp.dot(xb, wgb_ref[...], preferred_element_type=F32)
    ta = jnp.dot(oa_ref[...], wa_ref[...], preferred_element_type=F32)
    tb = jnp.dot(ob_ref[...], wb_ref[...], preferred_element_type=F32)
    merged = jax.nn.sigmoid(ga) * ta + jax.nn.sigmoid(gb) * tb
    mo = jnp.dot(merged.astype(BF16), wo_ref[...], preferred_element_type=F32)
    h_ref[...] = _layer_norm(DEEPNORM_ALPHA * x + mo, g_ref[...], b_ref[...])


def _post_attn(x2d, oa, ob, wga, wgb, wa, wb, wo, g, b):
    n_tok = x2d.shape[0]
    tm = POST_TM
    row = lambda i: (i, 0)
    const = lambda i: (0, 0)
    wspec = _resident((D_MODEL, D_MODEL), const)
    vspec = pl.BlockSpec((1, D_MODEL), const)
    return pl.pallas_call(
        _post_attn_kernel,
        grid=(n_tok // tm,),
        in_specs=[
            pl.BlockSpec((tm, D_MODEL), row),
            pl.BlockSpec((tm, D_MODEL), row),
            pl.BlockSpec((tm, D_MODEL), row),
            wspec, wspec, wspec, wspec, wspec, vspec, vspec,
        ],
        out_specs=pl.BlockSpec((tm, D_MODEL), row),
        out_shape=jax.ShapeDtypeStruct((n_tok, D_MODEL), F32),
        compiler_params=_params(1),
        name="post_attn",
    )(x2d, oa, ob, wga, wgb, wa, wb, wo, g, b)


def _ffn_kernel(h_ref, wg_ref, wu_ref, wd_ref, g_ref, b_ref, y_ref):
    h = h_ref[...]
    hb = h.astype(BF16)
    gate = jnp.dot(hb, wg_ref[...], preferred_element_type=F32)
    up = jnp.dot(hb, wu_ref[...], preferred_element_type=F32)
    act = jax.nn.silu(gate) * up
    f = jnp.dot(act.astype(BF16), wd_ref[...], preferred_element_type=F32)
    y_ref[...] = _layer_norm(DEEPNORM_ALPHA * h + f, g_ref[...], b_ref[...])


def _ffn(h2d, wg, wu, wd, g, b):
    n_tok = h2d.shape[0]
    d_ff = wg.shape[1]
    tm = FFN_TM
    row = lambda i: (i, 0)
    const = lambda i: (0, 0)
    vspec = pl.BlockSpec((1, D_MODEL), const)
    return pl.pallas_call(
        _ffn_kernel,
        grid=(n_tok // tm,),
        in_specs=[
            pl.BlockSpec((tm, D_MODEL), row),
            _resident((D_MODEL, d_ff), const),
            _resident((D_MODEL, d_ff), const),
            _resident((d_ff, D_MODEL), const),
            vspec, vspec,
        ],
        out_specs=pl.BlockSpec((tm, D_MODEL), row),
        out_shape=jax.ShapeDtypeStruct((n_tok, D_MODEL), F32),
        compiler_params=_params(1),
        name="ffn",
    )(h2d, wg, wu, wd, g, b)


def _t5_bucket(rel):
    nb = REL_BUCKETS // 2
    max_exact = nb // 2
    ret = (rel > 0).astype(jnp.int32) * nb
    n = jnp.abs(rel)
    nf = jnp.maximum(n, 1).astype(F32)
    large = max_exact + (jnp.log(nf / max_exact) / math.log(REL_MAX_DIST / max_exact)
                         * (nb - max_exact)).astype(jnp.int32)
    large = jnp.minimum(large, nb - 1)
    return ret + jnp.where(n < max_exact, n, large)


def _bucket_tiles():
    r = jnp.arange(ROW_BLOCK, dtype=jnp.int32)[None, :, None]
    c = jnp.arange(ATTN_A_TK, dtype=jnp.int32)[None, None, :]
    off = (jnp.arange(BIAS_TILES, dtype=jnp.int32) + (BIAS_OFF_LO - 1))[:, None, None] * ROW_BLOCK
    return _t5_bucket(c - r + off)


def _rope_tables(seq):
    tok = jnp.arange(seq)
    row = (tok // GRID_W).astype(F32)
    col = (tok % GRID_W).astype(F32)
    freqs = ROPE_THETA ** (-jnp.arange(0, ROPE_AXIS_DIM, 2, dtype=F32) / ROPE_AXIS_DIM)
    ang_r = row[:, None] * freqs[None, :]
    ang_c = col[:, None] * freqs[None, :]
    cos = jnp.concatenate([jnp.cos(ang_r), jnp.cos(ang_r), jnp.cos(ang_c), jnp.cos(ang_c)], axis=-1)
    sin = jnp.concatenate([-jnp.sin(ang_r), jnp.sin(ang_r), -jnp.sin(ang_c), jnp.sin(ang_c)], axis=-1)
    return cos, sin


def kernel(x_prompt, x_sample, w_in, lambda_q1, lambda_k1, lambda_q2, lambda_k2, subln_g,
           q_norm_g, k_norm_g, rel_bias, w_proj_a, w_proj_b, w_o, ln1_g, ln1_b,
           w_gate, w_up, w_down, ln2_g, ln2_b):
    assert w_in.shape[0] == DEPTH == 1
    lam_init = 0.8 - 0.6 * math.exp(-0.3 * 0)

    w_main = w_in[0][:, :PROJ_COLS].astype(BF16)
    wga = w_in[0][:, PROJ_COLS:PROJ_COLS + D_MODEL].astype(BF16)
    wgb = w_in[0][:, PROJ_COLS + D_MODEL:].astype(BF16)
    wa = w_proj_a[0].astype(BF16)
    wb = w_proj_b[0].astype(BF16)
    wo = w_o[0].astype(BF16)
    wg = w_gate[0].astype(BF16)
    wu = w_up[0].astype(BF16)
    wd = w_down[0].astype(BF16)

    col_scale = jnp.concatenate([jnp.full((1, A_Q_COLS), A_Q_SCALE, F32),
                                 jnp.ones((1, PROJ_A_COLS - A_Q_COLS), F32)], axis=1)
    bias = _bias_tiles(rel_bias, _bucket_tiles())

    def run(x):
        batch, seq, _ = x.shape
        x2d = x.reshape(batch * seq, D_MODEL)
        cos, sin = _rope_tables(seq)
        proj_a, proj_b = _in_proj(x2d, w_main, col_scale, cos, sin, q_norm_g, k_norm_g, seq)
        oa = _attn_a(proj_a, bias, lambda_q1, lambda_k1, lambda_q2, lambda_k2, subln_g,
                     batch, seq, lam_init)
        ob = _attn_b(proj_b, batch, seq)
        h = _post_attn(x2d, oa, ob, wga, wgb, wa, wb, wo, ln1_g, ln1_b)
        y = _ffn(h, wg, wu, wd, ln2_g, ln2_b)
        return y.reshape(batch, seq, D_MODEL)

    return (run(x_prompt), run(x_sample))
```

```python
import functools
import math

import jax
import jax.numpy as jnp
from jax import lax
from jax.experimental import pallas as pl
from jax.experimental.pallas import tpu as pltpu

F32 = jnp.float32
BF16 = jnp.bfloat16

D_MODEL = 1024
GRID_W = 64
HEAD_W = 128
A_HEADS = 8
A_QK_DIM = 64
B_HEADS = 8
B_KV_HEADS = 2
B_GROUP = B_HEADS // B_KV_HEADS
ROPE_THETA = 10000.0
ROPE_AXIS_DIM = HEAD_W // 2
REL_BUCKETS = 32
REL_MAX_DIST = 128
LN_EPS = 1e-5
RMS_EPS = 1e-6
DEPTH = 1
DEEPNORM_ALPHA = (2 * DEPTH) ** 0.25
LOG2E = math.log2(math.e)
A_Q_SCALE = A_QK_DIM ** -0.5 * LOG2E
B_Q_SCALE = HEAD_W ** -0.5 * LOG2E

A_Q_COLS = A_HEADS * HEAD_W
PROJ_A_COLS = 3 * A_Q_COLS
PROJ_B_COLS = (B_HEADS + 2 * B_KV_HEADS) * HEAD_W
PROJ_COLS = PROJ_A_COLS + PROJ_B_COLS
PROJ_TN = PROJ_B_COLS

VMEM_LIMIT_BYTES = 56 * 1024 * 1024

PROJ_TM = 512
ATTN_A_T = 2048
ATTN_A_TK = 512
ATTN_B_T = 1024
ATTN_B_TK = 512
POST_TM = 512
FFN_TM = 512
ROW_BLOCK = 128
MAX_CHUNK_UNROLL = 5

BIAS_OFF_HI = (ROW_BLOCK + REL_MAX_DIST - 2) // ROW_BLOCK
BIAS_OFF_LO = -((ATTN_A_TK + REL_MAX_DIST - 2) // ROW_BLOCK)
BIAS_TILES = BIAS_OFF_HI - BIAS_OFF_LO + 3


def _params(n_axes):
    return pltpu.CompilerParams(dimension_semantics=("arbitrary",) * n_axes,
                                vmem_limit_bytes=VMEM_LIMIT_BYTES)


def _resident(shape, index_map):
    return pl.BlockSpec(shape, index_map, pipeline_mode=pl.Buffered(1))


def _in_proj_kernel(x_ref, w_ref, cs_ref, cos_ref, sin_ref, gq_ref, gk_ref, oa_ref, ob_ref):
    xb = x_ref[...].astype(BF16)

    def matmul(c0, c1):
        return jnp.dot(xb, w_ref[:, c0:c1], preferred_element_type=F32)

    res_b = matmul(PROJ_A_COLS, PROJ_COLS)
    for c0 in range(0, PROJ_A_COLS, PROJ_TN):
        c1 = c0 + PROJ_TN
        res_a = (matmul(c0, c1) * cs_ref[:, c0:c1]).astype(BF16)
        for h in range(PROJ_TN // HEAD_W):
            oa_ref[c0 // HEAD_W + h] = res_a[:, h * HEAD_W:(h + 1) * HEAD_W]

    cos = cos_ref[...]
    sin = sin_ref[...]
    lane = lax.broadcasted_iota(jnp.int32, cos.shape, 1)
    first_half = (lane % ROPE_AXIS_DIM) < (ROPE_AXIS_DIM // 2)
    n_rope = B_HEADS + B_KV_HEADS
    for h in range(PROJ_B_COLS // HEAD_W):
        v = res_b[:, h * HEAD_W:(h + 1) * HEAD_W]
        if h < n_rope:
            g = gq_ref[...] if h < B_HEADS else gk_ref[...]
            ms = jnp.mean(v * v, axis=-1, keepdims=True)
            vn = v * lax.rsqrt(ms + RMS_EPS) * g
            rot = jnp.where(first_half,
                            pltpu.roll(vn, HEAD_W - ROPE_AXIS_DIM // 2, 1),
                            pltpu.roll(vn, ROPE_AXIS_DIM // 2, 1))
            v = vn * cos + rot * sin
            if h < B_HEADS:
                v = v * B_Q_SCALE
        ob_ref[h] = v.astype(BF16)


def _in_proj(x2d, w, col_scale, cos, sin, gq, gk, seq):
    n_tok = x2d.shape[0]
    tm = PROJ_TM
    tiles_per_seq = seq // tm
    row = lambda i: (i, 0)
    const = lambda i: (0, 0)
    return pl.pallas_call(
        _in_proj_kernel,
        grid=(n_tok // tm,),
        in_specs=[
            pl.BlockSpec((tm, D_MODEL), row),
            _resident((D_MODEL, PROJ_COLS), const),
            pl.BlockSpec((1, PROJ_A_COLS), const),
            pl.BlockSpec((tm, HEAD_W), lambda i: (i % tiles_per_seq, 0)),
            pl.BlockSpec((tm, HEAD_W), lambda i: (i % tiles_per_seq, 0)),
            pl.BlockSpec((1, HEAD_W), const),
            pl.BlockSpec((1, HEAD_W), const),
        ],
        out_specs=[pl.BlockSpec((PROJ_A_COLS // HEAD_W, tm, HEAD_W), lambda i: (0, i, 0)),
                   pl.BlockSpec((PROJ_B_COLS // HEAD_W, tm, HEAD_W), lambda i: (0, i, 0))],
        out_shape=[jax.ShapeDtypeStruct((PROJ_A_COLS // HEAD_W, n_tok, HEAD_W), BF16),
                   jax.ShapeDtypeStruct((PROJ_B_COLS // HEAD_W, n_tok, HEAD_W), BF16)],
        compiler_params=_params(1),
        name="in_proj",
    )(x2d, w, col_scale, cos, sin, gq, gk)


def _bias_tiles_kernel(rb_ref, bucket_ref, o_ref):
    h = pl.program_id(0)
    bucket = bucket_ref[0]
    acc = jnp.zeros(bucket.shape, F32)
    for b in range(REL_BUCKETS):
        acc = jnp.where(bucket == b, rb_ref[b, h], acc)
    o_ref[0, 0] = acc * LOG2E


def _bias_tiles(rel_bias, bucket):
    n_tiles, rb, tk = bucket.shape
    return pl.pallas_call(
        _bias_tiles_kernel,
        grid=(A_HEADS, n_tiles),
        in_specs=[
            pl.BlockSpec(memory_space=pltpu.SMEM),
            pl.BlockSpec((1, rb, tk), lambda h, d: (d, 0, 0)),
        ],
        out_specs=pl.BlockSpec((1, 1, rb, tk), lambda h, d: (h, d, 0, 0)),
        out_shape=jax.ShapeDtypeStruct((A_HEADS, n_tiles, rb, tk), F32),
        compiler_params=_params(2),
        name="t5_bias_tiles",
    )(rel_bias, bucket)


def _online_softmax_loop(q, k_ref, v_ref, n_chunks, tk, score_fn, m_ref, acc_ref):
    n_rows = q.shape[0]
    lane_blocks = tk // HEAD_W
    ones = jnp.ones((tk, HEAD_W), BF16)

    def chunk(j, first):
        start = pl.multiple_of(j * tk, tk)
        ks = k_ref[pl.ds(start, tk), :]
        v_ext = jnp.concatenate([v_ref[pl.ds(start, tk), :], ones], axis=1)
        s_all = lax.dot_general(q, ks, (((1,), (1,)), ((), ())), preferred_element_type=F32)
        for r0 in range(0, n_rows, ROW_BLOCK):
            rows = slice(r0, r0 + ROW_BLOCK)
            s = score_fn(s_all[rows], j, r0)
            m_cur = jnp.max(s, axis=-1, keepdims=True)
            if first:
                m_new = jnp.broadcast_to(m_cur, (ROW_BLOCK, HEAD_W))
            else:
                m_prev = m_ref[rows, :]
                m_new = jnp.maximum(m_prev, m_cur)
            p = jnp.exp2((s - jnp.tile(m_new, (1, lane_blocks))).astype(BF16))
            pv = jnp.dot(p, v_ext, preferred_element_type=F32)
            if first:
                acc_ref[rows, :] = pv
            else:
                alpha = jnp.exp2(m_prev - m_new)
                acc_ref[rows, :] = jnp.tile(alpha, (1, 2)) * acc_ref[rows, :] + pv
            m_ref[rows, :] = m_new

    chunk(0, True)
    rest = n_chunks - 1
    unroll = max(u for u in range(1, MAX_CHUNK_UNROLL + 1) if rest % u == 0)

    def body(j, carry):
        chunk(j, False)
        return carry

    lax.fori_loop(1, n_chunks, body, 0, unroll=unroll)


def _q_tile(full_tile, seq, tk):
    return min(seq, full_tile if seq // tk - 1 <= MAX_CHUNK_UNROLL else full_tile // 2)


def _softmax_scratch(n_rows):
    return [
        pltpu.VMEM((n_rows, HEAD_W), F32),
        pltpu.VMEM((n_rows, 2 * HEAD_W), F32),
    ]


def _attn_a_kernel(q_ref, k_ref, v_ref, bias_ref, lq1_ref, lk1_ref, lq2_ref, lk2_ref, g_ref,
                   o_ref, m_ref, acc_ref, *, n_chunks, tk, lam_init):
    t = q_ref.shape[0]
    i = pl.program_id(2)
    q = q_ref[...]
    lane = lax.broadcasted_iota(jnp.int32, q.shape, 1)
    zero = jnp.zeros_like(q)
    q2m = jnp.concatenate([jnp.where(lane < A_QK_DIM, q, zero),
                           jnp.where(lane >= A_QK_DIM, q, zero)], axis=0)

    def score_fn(s, j, r0):
        off = j * (tk // ROW_BLOCK) - (i * (t // ROW_BLOCK) + (r0 % t) // ROW_BLOCK)
        d = jnp.clip(off, BIAS_OFF_LO - 1, BIAS_OFF_HI + 1) - (BIAS_OFF_LO - 1)
        return s + bias_ref[0, d]

    _online_softmax_loop(q2m, k_ref, v_ref, n_chunks, tk, score_fn, m_ref, acc_ref)

    lam = (jnp.exp(jnp.sum(lq1_ref[...] * lk1_ref[...], axis=-1, keepdims=True))
           - jnp.exp(jnp.sum(lq2_ref[...] * lk2_ref[...], axis=-1, keepdims=True))
           + lam_init)
    o1 = acc_ref[0:t, 0:HEAD_W] / acc_ref[0:t, HEAD_W:2 * HEAD_W]
    o2 = acc_ref[t:2 * t, 0:HEAD_W] / acc_ref[t:2 * t, HEAD_W:2 * HEAD_W]
    o = o1 - lam * o2
    ms = jnp.mean(o * o, axis=-1, keepdims=True)
    o = (o * lax.rsqrt(ms + RMS_EPS) * g_ref[...]) * (1.0 - lam_init)
    o_ref[...] = o.astype(o_ref.dtype)


def _attn_a(proj, bias, lq1, lk1, lq2, lk2, subln_g, batch, seq, lam_init):
    tk = ATTN_A_TK
    t = _q_tile(ATTN_A_T, seq, tk)
    qt = seq // t
    lam_spec = pl.BlockSpec((1, A_QK_DIM), lambda b, h, i: (0, 0))
    return pl.pallas_call(
        functools.partial(_attn_a_kernel, n_chunks=seq // tk, tk=tk, lam_init=lam_init),
        grid=(batch, A_HEADS, qt),
        in_specs=[
            pl.BlockSpec((None, t, HEAD_W), lambda b, h, i: (h, b * qt + i, 0)),
            pl.BlockSpec((None, seq, HEAD_W), lambda b, h, i: (A_HEADS + h, b, 0)),
            pl.BlockSpec((None, seq, HEAD_W), lambda b, h, i: (2 * A_HEADS + h, b, 0)),
            pl.BlockSpec((1, BIAS_TILES, ROW_BLOCK, tk), lambda b, h, i: (h, 0, 0, 0)),
            lam_spec, lam_spec, lam_spec, lam_spec,
            pl.BlockSpec((1, HEAD_W), lambda b, h, i: (0, 0)),
        ],
        out_specs=pl.BlockSpec((t, HEAD_W), lambda b, h, i: (b * qt + i, h)),
        out_shape=jax.ShapeDtypeStruct((batch * seq, A_HEADS * HEAD_W), BF16),
        scratch_shapes=_softmax_scratch(2 * t),
        compiler_params=_params(3),
        name="attn_a",
    )(proj, proj, proj, bias, lq1, lk1, lq2, lk2, subln_g)


def _attn_b_kernel(q_ref, k_ref, v_ref, o_ref, m_ref, acc_ref, *, n_chunks, tk):
    t = q_ref.shape[1]
    q = q_ref[...].reshape(B_GROUP * t, HEAD_W)

    _online_softmax_loop(q, k_ref, v_ref, n_chunks, tk, lambda s, j, r0: s,
                         m_ref, acc_ref)

    o = acc_ref[:, 0:HEAD_W] / acc_ref[:, HEAD_W:2 * HEAD_W]
    for h in range(B_GROUP):
        o_ref[:, h * HEAD_W:(h + 1) * HEAD_W] = o[h * t:(h + 1) * t, :].astype(o_ref.dtype)


def _attn_b(proj, batch, seq):
    tk = ATTN_B_TK
    t = _q_tile(ATTN_B_T, seq, tk)
    qt = seq // t
    gw = B_GROUP * HEAD_W
    return pl.pallas_call(
        functools.partial(_attn_b_kernel, n_chunks=seq // tk, tk=tk),
        grid=(batch, B_KV_HEADS, qt),
        in_specs=[
            pl.BlockSpec((B_GROUP, t, HEAD_W), lambda b, g, i: (g, b * qt + i, 0)),
            pl.BlockSpec((None, seq, HEAD_W), lambda b, g, i: (B_HEADS + g, b, 0)),
            pl.BlockSpec((None, seq, HEAD_W), lambda b, g, i: (B_HEADS + B_KV_HEADS + g, b, 0)),
        ],
        out_specs=pl.BlockSpec((t, gw), lambda b, g, i: (b * qt + i, g)),
        out_shape=jax.ShapeDtypeStruct((batch * seq, B_HEADS * HEAD_W), BF16),
        scratch_shapes=_softmax_scratch(B_GROUP * t),
        compiler_params=_params(3),
        name="attn_b",
    )(proj, proj, proj)


def _layer_norm(x, g, b):
    mu = jnp.mean(x, axis=-1, keepdims=True)
    xc = x - mu
    var = jnp.mean(xc * xc, axis=-1, keepdims=True)
    return xc * lax.rsqrt(var + LN_EPS) * g + b


def _post_attn_kernel(x_ref, oa_ref, ob_ref, wga_ref, wgb_ref, wa_ref, wb_ref, wo_ref,
                      g_ref, b_ref, h_ref):
    x = x_ref[...]
    xb = x.astype(BF16)
    ga = jnp.dot(xb, wga_ref[...], preferred_element_type=F32)
    gb = jnp.dot(xb, wgb_ref[...], preferred_element_type=F32)
    ta = jnp.dot(oa_ref[...], wa_ref[...], preferred_element_type=F32)
    tb = jnp.dot(ob_ref[...], wb_ref[...], preferred_element_type=F32)
    merged = jax.nn.sigmoid(ga) * ta + jax.nn.sigmoid(gb) * tb
    mo = jnp.dot(merged.astype(BF16), wo_ref[...], preferred_element_type=F32)
    h_ref[...] = _layer_norm(DEEPNORM_ALPHA * x + mo, g_ref[...], b_ref[...])


def _post_attn(x2d, oa, ob, wga, wgb, wa, wb, wo, g, b):
    n_tok = x2d.shape[0]
    tm = POST_TM
    row = lambda i: (i, 0)
    const = lambda i: (0, 0)
    wspec = _resident((D_MODEL, D_MODEL), const)
    vspec = pl.BlockSpec((1, D_MODEL), const)
    return pl.pallas_call(
        _post_attn_kernel,
        grid=(n_tok // tm,),
        in_specs=[
            pl.BlockSpec((tm, D_MODEL), row),
            pl.BlockSpec((tm, D_MODEL), row),
            pl.BlockSpec((tm, D_MODEL), row),
            wspec, wspec, wspec, wspec, wspec, vspec, vspec,
        ],
        out_specs=pl.BlockSpec((tm, D_MODEL), row),
        out_shape=jax.ShapeDtypeStruct((n_tok, D_MODEL), F32),
        compiler_params=_params(1),
        name="post_attn",
    )(x2d, oa, ob, wga, wgb, wa, wb, wo, g, b)


def _ffn_kernel(h_ref, wg_ref, wu_ref, wd_ref, g_ref, b_ref, y_ref):
    h = h_ref[...]
    hb = h.astype(BF16)
    gate = jnp.dot(hb, wg_ref[...], preferred_element_type=F32)
    up = jnp.dot(hb, wu_ref[...], preferred_element_type=F32)
    act = jax.nn.silu(gate) * up
    f = jnp.dot(act.astype(BF16), wd_ref[...], preferred_element_type=F32)
    y_ref[...] = _layer_norm(DEEPNORM_ALPHA * h + f, g_ref[...], b_ref[...])


def _ffn(h2d, wg, wu, wd, g, b):
    n_tok = h2d.shape[0]
    d_ff = wg.shape[1]
    tm = FFN_TM
    row = lambda i: (i, 0)
    const = lambda i: (0, 0)
    vspec = pl.BlockSpec((1, D_MODEL), const)
    return pl.pallas_call(
        _ffn_kernel,
        grid=(n_tok // tm,),
        in_specs=[
            pl.BlockSpec((tm, D_MODEL), row),
            _resident((D_MODEL, d_ff), const),
            _resident((D_MODEL, d_ff), const),
            _resident((d_ff, D_MODEL), const),
            vspec, vspec,
        ],
        out_specs=pl.BlockSpec((tm, D_MODEL), row),
        out_shape=jax.ShapeDtypeStruct((n_tok, D_MODEL), F32),
        compiler_params=_params(1),
        name="ffn",
    )(h2d, wg, wu, wd, g, b)


def _t5_bucket(rel):
    nb = REL_BUCKETS // 2
    max_exact = nb // 2
    ret = (rel > 0).astype(jnp.int32) * nb
    n = jnp.abs(rel)
    nf = jnp.maximum(n, 1).astype(F32)
    large = max_exact + (jnp.log(nf / max_exact) / math.log(REL_MAX_DIST / max_exact)
                         * (nb - max_exact)).astype(jnp.int32)
    large = jnp.minimum(large, nb - 1)
    return ret + jnp.where(n < max_exact, n, large)


def _bucket_tiles():
    r = jnp.arange(ROW_BLOCK, dtype=jnp.int32)[None, :, None]
    c = jnp.arange(ATTN_A_TK, dtype=jnp.int32)[None, None, :]
    off = (jnp.arange(BIAS_TILES, dtype=jnp.int32) + (BIAS_OFF_LO - 1))[:, None, None] * ROW_BLOCK
    return _t5_bucket(c - r + off)


def _rope_tables(seq):
    tok = jnp.arange(seq)
    row = (tok // GRID_W).astype(F32)
    col = (tok % GRID_W).astype(F32)
    freqs = ROPE_THETA ** (-jnp.arange(0, ROPE_AXIS_DIM, 2, dtype=F32) / ROPE_AXIS_DIM)
    ang_r = row[:, None] * freqs[None, :]
    ang_c = col[:, None] * freqs[None, :]
    cos = jnp.concatenate([jnp.cos(ang_r), jnp.cos(ang_r), jnp.cos(ang_c), jnp.cos(ang_c)], axis=-1)
    sin = jnp.concatenate([-jnp.sin(ang_r), jnp.sin(ang_r), -jnp.sin(ang_c), jnp.sin(ang_c)], axis=-1)
    return cos, sin


def kernel(x_prompt, x_sample, w_in, lambda_q1, lambda_k1, lambda_q2, lambda_k2, subln_g,
           q_norm_g, k_norm_g, rel_bias, w_proj_a, w_proj_b, w_o, ln1_g, ln1_b,
           w_gate, w_up, w_down, ln2_g, ln2_b):
    assert w_in.shape[0] == DEPTH == 1
    lam_init = 0.8 - 0.6 * math.exp(-0.3 * 0)

    w_main = w_in[0][:, :PROJ_COLS].astype(BF16)
    wga = w_in[0][:, PROJ_COLS:PROJ_COLS + D_MODEL].astype(BF16)
    wgb = w_in[0][:, PROJ_COLS + D_MODEL:].astype(BF16)
    wa = w_proj_a[0].astype(BF16)
    wb = w_proj_b[0].astype(BF16)
    wo = w_o[0].astype(BF16)
    wg = w_gate[0].astype(BF16)
    wu = w_up[0].astype(BF16)
    wd = w_down[0].astype(BF16)

    col_scale = jnp.concatenate([jnp.full((1, A_Q_COLS), A_Q_SCALE, F32),
                                 jnp.ones((1, PROJ_A_COLS - A_Q_COLS), F32)], axis=1)
    bias = _bias_tiles(rel_bias, _bucket_tiles())

    def run(x):
        batch, seq, _ = x.shape
        x2d = x.reshape(batch * seq, D_MODEL)
        cos, sin = _rope_tables(seq)
        proj_a, proj_b = _in_proj(x2d, w_main, col_scale, cos, sin, q_norm_g, k_norm_g, seq)
        oa = _attn_a(proj_a, bias, lambda_q1, lambda_k1, lambda_q2, lambda_k2, subln_g,
                     batch, seq, lam_init)
        ob = _attn_b(proj_b, batch, seq)
        h = _post_attn(x2d, oa, ob, wga, wgb, wa, wb, wo, ln1_g, ln1_b)
        y = _ffn(h, wg, wu, wd, ln2_g, ln2_b)
        return y.reshape(batch, seq, D_MODEL)

    return (run(x_prompt), run(x_sample))
```
